```python
import jax
import jax.numpy as jnp
from jax import lax
import numpy as np

D_MODEL = 4096
BATCH = 8
SEQ = 2048
DEPTH = 4
DEC_BATCH = 16
DEC_SEQ = 32
PAST_LEN = 1024

CHUNK = 64
A_HEADS = 8
A_DK = 128
A_DV = 128
D_A = A_HEADS * A_DK
B_GROUPS = 8
B_DG = 128
D_B = B_GROUPS * B_DG
GMLP_CHUNK = 128
C_HEADS = 16
C_DH = 128
D_C = C_HEADS * C_DH
Q_BLOCK = 128
D_FF = 11008
EPS = 1e-6
IN_SPLITS = (D_A, 2 * D_A, 3 * D_A, 4 * D_A,
             4 * D_A + D_B, 4 * D_A + 2 * D_B,
             4 * D_A + 2 * D_B + D_C, 4 * D_A + 2 * D_B + 2 * D_C,
             4 * D_A + 2 * D_B + 3 * D_C)
N_IN = 4 * D_A + 2 * D_B + 3 * D_C + C_HEADS
N_BRANCH = 3

kernel_name = 'hybrid_streaming_encoder_step'


def rms_norm(x, g):
    xf = x.astype(jnp.float32)
    y = xf * lax.rsqrt(jnp.mean(xf * xf, axis=-1, keepdims=True) + EPS)
    return (y * g.astype(jnp.float32)).astype(x.dtype)


def swiglu_ffn(x, w_in, w_out):
    gate, up = jnp.split(x @ w_in, 2, axis=-1)
    return (jax.nn.silu(gate) * up) @ w_out


def hgrn2_chunk(S, xs):
    q, k, i, logf = xs
    c = q.shape[1]
    b = jnp.cumsum(logf, axis=1)
    o_inter = jnp.einsum('bthk,bhkv->bthv', q * jnp.exp(b), S)
    causal = jnp.tril(jnp.ones((c, c), dtype=bool))[None, :, :, None, None]
    diff = b[:, :, None] - b[:, None, :]
    decay = jnp.exp(jnp.where(causal, diff, -jnp.inf))
    scores = jnp.einsum('bthk,bshk,btshk->bhts', q, k, decay)
    o_intra = jnp.einsum('bhts,bshv->bthv', scores, i)
    b_last = b[:, -1]
    S_new = jnp.exp(b_last)[..., None] * S + jnp.einsum(
        'bshk,bshv->bhkv', k * jnp.exp(b_last[:, None] - b), i)
    return S_new, o_inter + o_intra


def hgrn2_mixer(q_raw, f_raw, i_raw, g_raw, lb, norm_g, s0):
    B, T, _ = q_raw.shape
    heads = lambda z: z.reshape(B, T, A_HEADS, A_DK).astype(jnp.float32)
    f = lb + (1.0 - lb) * jax.nn.sigmoid(f_raw.astype(jnp.float32))
    logf = heads(jnp.log(f))
    k = heads(1.0 - f)
    q = heads(jax.nn.silu(q_raw))
    i = heads(i_raw)
    c = min(T, CHUNK)
    n = T // c
    to_chunks = lambda z: jnp.moveaxis(z.reshape(B, n, c, A_HEADS, A_DK), 1, 0)
    s_final, o = lax.scan(hgrn2_chunk, s0.astype(jnp.float32),
                          (to_chunks(q), to_chunks(k), to_chunks(i), to_chunks(logf)))
    o = jnp.moveaxis(o, 0, 1).reshape(B, T, A_HEADS, A_DV)
    o = o * lax.rsqrt(jnp.mean(o * o, axis=-1, keepdims=True) + EPS)
    o = o * norm_g.astype(jnp.float32).reshape(A_HEADS, A_DV)
    o = o.reshape(B, T, D_A) * jax.nn.silu(g_raw.astype(jnp.float32))
    return o.astype(q_raw.dtype), s_final


def chunk_mlp_mixer(u, v, norm_g, w_s, b_s):
    B, T, _ = u.shape
    v = rms_norm(v, norm_g)
    c = min(T, GMLP_CHUNK)
    n = T // c
    w = w_s[:, :c, :c] * jnp.tril(jnp.ones((c, c), dtype=w_s.dtype))
    vg = v.reshape(B, n, c, B_GROUPS, B_DG)
    s = jnp.einsum('gts,bnsgd->bntgd', w, vg) + jnp.transpose(b_s[:, :c])[:, :, None]
    return u * s.reshape(B, T, D_B), v


def fox_attention(q, k, v, F, q_offset):
    B, Tq, H, Dh = q.shape
    Tk = k.shape[1]
    qb = min(Tq, Q_BLOCK)
    nb = Tq // qb
    k_pos = jnp.arange(Tk)
    F_keys = jnp.moveaxis(F, 1, 2)
    scale = Dh ** -0.5

    def one_block(j):
        start = j * qb
        q_blk = lax.dynamic_slice_in_dim(q, start, qb, axis=1)
        F_q = lax.dynamic_slice_in_dim(F, q_offset + start, qb, axis=1)
        s = jnp.einsum('bqhd,bkhd->bhqk', q_blk, k,
                       preferred_element_type=jnp.float32) * scale
        s = s + jnp.moveaxis(F_q, 1, 2)[..., None] - F_keys[:, :, None, :]
        q_pos = q_offset + start + jnp.arange(qb)
        s = jnp.where(q_pos[:, None] >= k_pos[None, :], s, -jnp.inf)
        p = jax.nn.softmax(s, axis=-1)
        return jnp.einsum('bhqk,bkhd->bqhd', p.astype(v.dtype), v)

    o = lax.map(one_block, jnp.arange(nb))
    return jnp.moveaxis(o, 0, 1).reshape(B, Tq, H, Dh)


def trunk_layer(x, s_a0, past, lb, ffn1_norm, ffn1_w_in, ffn1_w_out, mix_norm,
                w_in, w_gate, a_norm, b_norm, w_s, b_s, fox_bias,
                w_branch_a, w_branch_b, w_branch_c, w_out,
                ffn2_norm, ffn2_w_in, ffn2_w_out):
    h = x + 0.5 * swiglu_ffn(rms_norm(x, ffn1_norm), ffn1_w_in, ffn1_w_out)
    n = rms_norm(h, mix_norm)
    B, T, _ = n.shape
    a_q, a_f, a_i, a_g, b_u, b_v, c_q, c_k, c_v, c_f = jnp.split(n @ w_in, IN_SPLITS, axis=-1)
    o_a, s_a = hgrn2_mixer(a_q, a_f, a_i, a_g, lb, a_norm, s_a0)
    o_b, v_b = chunk_mlp_mixer(b_u, b_v, b_norm, w_s, b_s)
    heads_c = lambda z: z.reshape(B, T, C_HEADS, C_DH)
    k_c, v_c = heads_c(c_k), heads_c(c_v)
    logf = jax.nn.log_sigmoid(c_f.astype(jnp.float32) + fox_bias.astype(jnp.float32))
    if past is None:
        k_all, v_all, logf_all, q_off = k_c, v_c, logf, 0
    else:
        pk, pv, plf = past
        k_all = jnp.concatenate([pk.astype(k_c.dtype), k_c], axis=1)
        v_all = jnp.concatenate([pv.astype(v_c.dtype), v_c], axis=1)
        logf_all = jnp.concatenate([plf.astype(jnp.float32), logf], axis=1)
        q_off = pk.shape[1]
    F = jnp.cumsum(logf_all, axis=1)
    o_c = fox_attention(heads_c(c_q), k_all, v_all, F, q_off).reshape(B, T, D_C)
    g_a, g_b, g_c = jnp.split(jax.nn.sigmoid(n @ w_gate), N_BRANCH, axis=-1)
    merged = g_a * (o_a @ w_branch_a) + g_b * (o_b @ w_branch_b) + g_c * (o_c @ w_branch_c)
    h = h + merged @ w_out
    y = h + 0.5 * swiglu_ffn(rms_norm(h, ffn2_norm), ffn2_w_in, ffn2_w_out)
    return y, s_a, v_b, k_c, v_c, logf


def setup_inputs(seed: int = 0) -> dict:
    key = jax.random.key(seed)
    ks = jax.random.split(key, 32)
    nrm = lambda k, shape, scale: jax.random.normal(k, shape, jnp.float32) * scale
    gain = lambda k, shape: 1.0 + 0.02 * jax.random.normal(k, shape, jnp.float32)
    return {
        'x_prompt': nrm(ks[0], (BATCH, SEQ, D_MODEL), 1.0),
        'x_sample': nrm(ks[1], (DEC_BATCH, DEC_SEQ, D_MODEL), 1.0),
        'state_hgrn': nrm(ks[2], (DEPTH, DEC_BATCH, A_HEADS, A_DK, A_DV), 0.5),
        'cache_k': nrm(ks[3], (DEPTH, DEC_BATCH, PAST_LEN, C_HEADS, C_DH), 1.0),
        'cache_v': nrm(ks[4], (DEPTH, DEC_BATCH, PAST_LEN, C_HEADS, C_DH), 1.0),
        'cache_logf': jax.nn.log_sigmoid(2.0 + nrm(ks[5], (DEPTH, DEC_BATCH, PAST_LEN, C_HEADS), 1.0)),
        'ffn1_norm': gain(ks[6], (DEPTH, D_MODEL)),
        'ffn1_w_in': nrm(ks[7], (DEPTH, D_MODEL, 2 * D_FF), D_MODEL ** -0.5),
        'ffn1_w_out': nrm(ks[8], (DEPTH, D_FF, D_MODEL), D_FF ** -0.5),
        'mix_norm': gain(ks[9], (DEPTH, D_MODEL)),
        'w_in': nrm(ks[10], (DEPTH, D_MODEL, N_IN), D_MODEL ** -0.5),
        'w_gate': nrm(ks[11], (DEPTH, D_MODEL, N_BRANCH * D_MODEL), D_MODEL ** -0.5),
        'hgrn_lb_logits': nrm(ks[12], (DEPTH, D_A), 0.5),
        'hgrn_norm': gain(ks[13], (DEPTH, D_A)),
        'gmlp_norm': gain(ks[14], (DEPTH, D_B)),
        'gmlp_w_s': nrm(ks[15], (DEPTH, B_GROUPS, GMLP_CHUNK, GMLP_CHUNK), GMLP_CHUNK ** -0.5),
        'gmlp_b_s': gain(ks[16], (DEPTH, B_GROUPS, GMLP_CHUNK)),
        'fox_bias': 2.0 + nrm(ks[17], (DEPTH, C_HEADS), 0.5),
        'w_branch_a': nrm(ks[18], (DEPTH, D_A, D_MODEL), D_A ** -0.5),
        'w_branch_b': nrm(ks[19], (DEPTH, D_B, D_MODEL), D_B ** -0.5),
        'w_branch_c': nrm(ks[20], (DEPTH, D_C, D_MODEL), D_C ** -0.5),
        'w_out': nrm(ks[21], (DEPTH, D_MODEL, D_MODEL), D_MODEL ** -0.5),
        'ffn2_norm': gain(ks[22], (DEPTH, D_MODEL)),
        'ffn2_w_in': nrm(ks[23], (DEPTH, D_MODEL, 2 * D_FF), D_MODEL ** -0.5),
        'ffn2_w_out': nrm(ks[24], (DEPTH, D_FF, D_MODEL), D_FF ** -0.5),
        'final_norm': gain(ks[25], (D_MODEL,)),
    }


def reference(x_prompt, x_sample, state_hgrn, cache_k, cache_v, cache_logf,
              ffn1_norm, ffn1_w_in, ffn1_w_out, mix_norm, w_in, w_gate,
              hgrn_lb_logits, hgrn_norm, gmlp_norm, gmlp_w_s, gmlp_b_s, fox_bias,
              w_branch_a, w_branch_b, w_branch_c, w_out,
              ffn2_norm, ffn2_w_in, ffn2_w_out, final_norm):
    lb_w = jax.nn.softmax(hgrn_lb_logits.astype(jnp.float32), axis=0)
    lower_bounds = jnp.cumsum(lb_w, axis=0) - lb_w[0]
    hp, hs = x_prompt, x_sample
    sa_p_all, sa_s_all, vb_s_all = [], [], []
    k_p_all, k_s_all, v_p_all, v_s_all, lf_p_all, lf_s_all = [], [], [], [], [], []
    for l in range(DEPTH):
        shared = (lower_bounds[l], ffn1_norm[l], ffn1_w_in[l], ffn1_w_out[l], mix_norm[l],
                  w_in[l], w_gate[l], hgrn_norm[l], gmlp_norm[l], gmlp_w_s[l], gmlp_b_s[l],
                  fox_bias[l], w_branch_a[l], w_branch_b[l], w_branch_c[l], w_out[l],
                  ffn2_norm[l], ffn2_w_in[l], ffn2_w_out[l])
        s0_p = jnp.zeros((hp.shape[0], A_HEADS, A_DK, A_DV), jnp.float32)
        hp, sa_p, _, k_p, v_p, lf_p = trunk_layer(hp, s0_p, None, *shared)
        hs, sa_s, vb_s, k_s, v_s, lf_s = trunk_layer(
            hs, state_hgrn[l], (cache_k[l], cache_v[l], cache_logf[l]), *shared)
        sa_p_all.append(sa_p)
        sa_s_all.append(sa_s)
        vb_s_all.append(vb_s)
        k_p_all.append(k_p)
        k_s_all.append(k_s)
        v_p_all.append(v_p)
        v_s_all.append(v_s)
        lf_p_all.append(lf_p)
        lf_s_all.append(lf_s)
    y_prompt = rms_norm(hp, final_norm)
    y_sample = rms_norm(hs, final_norm)
    return (y_prompt, y_sample,
            jnp.stack(sa_p_all), jnp.stack(sa_s_all), jnp.stack(vb_s_all),
            jnp.stack(k_p_all), jnp.stack(k_s_all),
            jnp.stack(v_p_all), jnp.stack(v_s_all),
            jnp.stack(lf_p_all), jnp.stack(lf_s_all))
```

```python
import functools

import jax
import jax.numpy as jnp
from jax import lax
from jax.experimental import pallas as pl
from jax.experimental.pallas import tpu as pltpu

EPS = 1e-6
LANE = 128
HGRN_CHUNK = 64
HGRN_ROWS_PER_STEP = 256
GMLP_CHUNK = 128
ATTN_BLOCK = 256
VMEM_LIMIT_BYTES = 56 * 1024 * 1024

F32 = jnp.float32
BF16 = jnp.bfloat16
NT = (((1,), (1,)), ((), ()))
TN = (((0,), (0,)), ((), ()))


def _tile(n, target, align):
    t = (min(target, n) // align) * align
    while t >= align:
        if n % t == 0:
            return t
        t -= align
    return n


def _params(*sem):
    return pltpu.CompilerParams(dimension_semantics=sem, vmem_limit_bytes=VMEM_LIMIT_BYTES)


def _sigmoid(x):
    return jax.nn.sigmoid(x)


def _row_cumsum(x):
    n = x.shape[0]
    row = lax.broadcasted_iota(jnp.int32, x.shape, 0)
    sh = 1
    while sh < n:
        x = x + jnp.where(row >= sh, pltpu.roll(x, sh, 0), 0.0)
        sh *= 2
    return x


def _rmsnorm_kernel(x_ref, g_ref, o_ref):
    x = x_ref[...]
    y = x * lax.rsqrt(jnp.mean(x * x, axis=-1, keepdims=True) + EPS)
    o_ref[...] = (y * g_ref[...]).astype(o_ref.dtype)


def _rmsnorm(x, g, out_dtype):
    m, d = x.shape
    tm = _tile(m, 256, 16)
    return pl.pallas_call(
        _rmsnorm_kernel,
        grid=(m // tm,),
        in_specs=[pl.BlockSpec((tm, d), lambda i: (i, 0)),
                  pl.BlockSpec((1, d), lambda i: (0, 0))],
        out_specs=pl.BlockSpec((tm, d), lambda i: (i, 0)),
        out_shape=jax.ShapeDtypeStruct((m, d), out_dtype),
        compiler_params=_params("parallel"),
        name="rmsnorm",
    )(x, g.reshape(1, d))


def _ffn_up_kernel(a_ref, wg_ref, wu_ref, o_ref):
    a = a_ref[...]
    gate = jnp.dot(a, wg_ref[...], preferred_element_type=F32)
    up = jnp.dot(a, wu_ref[...], preferred_element_type=F32)
    o_ref[...] = (gate * _sigmoid(gate) * up).astype(o_ref.dtype)


def _ffn_up(a, w_in):
    m, d = a.shape
    f = w_in.shape[1] // 2
    tm = _tile(m, 768, 16)
    tn = _tile(f, 256, LANE)
    nf = f // tn
    return pl.pallas_call(
        _ffn_up_kernel,
        grid=(m // tm, nf),
        in_specs=[pl.BlockSpec((tm, d), lambda i, j: (i, 0)),
                  pl.BlockSpec((d, tn), lambda i, j: (0, j)),
                  pl.BlockSpec((d, tn), lambda i, j: (0, j + nf))],
        out_specs=pl.BlockSpec((tm, tn), lambda i, j: (i, j)),
        out_shape=jax.ShapeDtypeStruct((m, f), BF16),
        compiler_params=_params("parallel", "arbitrary"),
        name="ffn_up",
    )(a, w_in, w_in)


def _mm_res_kernel(a_ref, w_ref, r_ref, o_ref, *, scale):
    acc = jnp.dot(a_ref[...], w_ref[...], preferred_element_type=F32)
    o_ref[...] = r_ref[...] + scale * acc


def _mm_res(a, w, r, scale, tn_target):
    m, k = a.shape
    n = w.shape[1]
    tm = _tile(m, 768, 16)
    tn = _tile(n, tn_target, LANE)
    return pl.pallas_call(
        functools.partial(_mm_res_kernel, scale=scale),
        grid=(m // tm, n // tn),
        in_specs=[pl.BlockSpec((tm, k), lambda i, j: (i, 0)),
                  pl.BlockSpec((k, tn), lambda i, j: (0, j)),
                  pl.BlockSpec((tm, tn), lambda i, j: (i, j))],
        out_specs=pl.BlockSpec((tm, tn), lambda i, j: (i, j)),
        out_shape=jax.ShapeDtypeStruct((m, n), F32),
        compiler_params=_params("parallel", "arbitrary"),
        name="mm_res",
    )(a, w, r)


def _mm_kernel(a_ref, w_ref, o_ref, *, gate):
    acc = jnp.dot(a_ref[...], w_ref[...], preferred_element_type=F32)
    o_ref[...] = _sigmoid(acc) if gate else acc


def _mm(a, w, gate, tn_target=1024):
    m, k = a.shape
    n = w.shape[1]
    tm = _tile(m, 768, 16)
    tn = _tile(n, tn_target, LANE)
    return pl.pallas_call(
        functools.partial(_mm_kernel, gate=gate),
        grid=(m // tm, n // tn),
        in_specs=[pl.BlockSpec((tm, k), lambda i, j: (i, 0)),
                  pl.BlockSpec((k, tn), lambda i, j: (0, j))],
        out_specs=pl.BlockSpec((tm, tn), lambda i, j: (i, j)),
        out_shape=jax.ShapeDtypeStruct((m, n), F32),
        compiler_params=_params("parallel", "arbitrary"),
        name="mm_gate" if gate else "mm",
    )(a, w)


def _branch_kernel(oa_ref, ob_ref, oc_ref, wa_ref, wb_ref, wc_ref,
                   ga_ref, gb_ref, gc_ref, o_ref):
    dot = functools.partial(jnp.dot, preferred_element_type=F32)
    merged = (ga_ref[...] * dot(oa_ref[...], wa_ref[...])
              + gb_ref[...] * dot(ob_ref[...], wb_ref[...])
              + gc_ref[...] * dot(oc_ref[...], wc_ref[...]))
    o_ref[...] = merged.astype(o_ref.dtype)


def _branch_merge(oa, ob, oc, wa, wb, wc, g):
    m = oa.shape[0]
    d = wa.shape[1]
    tm = _tile(m, 768, 16)
    tn = _tile(d, 512, LANE)
    nd = d // tn
    a_spec = lambda x: pl.BlockSpec((tm, x.shape[1]), lambda i, j: (i, 0))
    w_spec = lambda x: pl.BlockSpec((x.shape[0], tn), lambda i, j: (0, j))
    g_spec = lambda b: pl.BlockSpec((tm, tn), lambda i, j: (i, j + b * nd))
    return pl.pallas_call(
        _branch_kernel,
        grid=(m // tm, nd),
        in_specs=[a_spec(oa), a_spec(ob), a_spec(oc), w_spec(wa), w_spec(wb), w_spec(wc),
                  g_spec(0), g_spec(1), g_spec(2)],
        out_specs=pl.BlockSpec((tm, tn), lambda i, j: (i, j)),
        out_shape=jax.ShapeDtypeStruct((m, d), BF16),
        compiler_params=_params("parallel", "arbitrary"),
        name="branch_merge",
    )(oa, ob, oc, wa, wb, wc, g, g, g)


def _lower_bound_kernel(x_ref, o_ref):
    x = x_ref[...]
    depth = x.shape[0]
    e = jnp.exp(x - jnp.max(x, axis=0, keepdims=True))
    w = e / jnp.sum(e, axis=0, keepdims=True)
    run = w[0:1, :]
    rows = [run - w[0:1, :]]
    for l in range(1, depth):
        run = run + w[l:l + 1, :]
        rows.append(run - w[0:1, :])
    o_ref[...] = jnp.concatenate(rows, axis=0)


def _lower_bounds(logits):
    return pl.pallas_call(
        _lower_bound_kernel,
        out_shape=jax.ShapeDtypeStruct(logits.shape, F32),
        name="hgrn_lower_bounds",
    )(logits.astype(F32))


def _hgrn_kernel(q_ref, f_ref, i_ref, g_ref, lb_ref, ng_ref, s0_ref,
                 o_ref, sout_ref, st_scr, *, c, cpb):
    j = pl.program_id(2)

    @pl.when(j == 0)
    def _():
        st_scr[...] = s0_ref[0, 0].T

    lb = lb_ref[...]
    ng = ng_ref[...]
    rowi = lax.broadcasted_iota(jnp.int32, (c, LANE), 0)
    ti = lax.broadcasted_iota(jnp.int32, (c, c), 0)
    si = lax.broadcasted_iota(jnp.int32, (c, c), 1)
    tx = ti ^ si
    eye = ti == si
    levels = []
    hb = 1
    while hb < c:
        levels.append(hb)
        hb *= 2
    masks = [(ti > si) & (tx >= hb) & (tx < 2 * hb) for hb in levels]
    dotf = functools.partial(lax.dot_general, preferred_element_type=F32)

    def level_ref(b, hb):
        n = 2 * hb
        if n == 2:
            return jnp.where((rowi & 1) != 0, pltpu.roll(b, 1, 0), b)
        if n == 4:
            m = rowi & 3
            return jnp.where(m == 0, pltpu.roll(b, c - 1, 0),
                             jnp.where(m == 1, b,
                                       jnp.where(m == 2, pltpu.roll(b, 1, 0),
                                                 pltpu.roll(b, 2, 0))))
        pieces = [jnp.broadcast_to(b[base + hb - 1:base + hb, :], (n, LANE))
                  for base in range(0, c, n)]
        return pieces[0] if len(pieces) == 1 else jnp.concatenate(pieces, axis=0)

    def chunk(ci, carry):
        rows = pl.ds(pl.multiple_of(ci * c, c), c)
        f = lb + (1.0 - lb) * _sigmoid(f_ref[rows, :])
        logf = jnp.log(f)
        kk = 1.0 - f
        qr = q_ref[rows, :]
        q = qr * _sigmoid(qr)
        iv = i_ref[rows, :].astype(BF16)
        b = _row_cumsum(logf)
        b_last = b[c - 1:c, :]
        st = st_scr[...]
        o = dotf((q * jnp.exp(b)).astype(BF16), st.astype(BF16), NT)
        a = jnp.where(eye, dotf(q.astype(BF16), kk.astype(BF16), NT), 0.0)
        for hb, mask in zip(levels, masks):
            ref = level_ref(b, hb)
            upper = (rowi & hb) != 0
            qt = q * jnp.exp(jnp.where(upper, b - ref, -jnp.inf))
            kt = kk * jnp.exp(jnp.where(upper, -jnp.inf, ref - b))
            a = a + jnp.where(mask, dotf(qt.astype(BF16), kt.astype(BF16), NT), 0.0)
        o = o + jnp.dot(a.astype(BF16), iv, preferred_element_type=F32)
        kb = kk * jnp.exp(b_last - b)
        st_scr[...] = st * jnp.exp(b_last) + dotf(iv, kb.astype(BF16), TN)
        o = o * lax.rsqrt(jnp.mean(o * o, axis=-1, keepdims=True) + EPS) * ng
        gr = g_ref[rows, :]
        o_ref[rows, :] = (o * (gr * _sigmoid(gr))).astype(o_ref.dtype)
        return carry

    lax.fori_loop(0, cpb, chunk, 0)

    @pl.when(j == pl.num_programs(2) - 1)
    def _():
        sout_ref[0, 0] = st_scr[...].T


def _hgrn(p, row0, nstream, t, heads, lb, norm_g, s0):
    c = min(t, HGRN_CHUNK)
    rows = _tile(t, HGRN_ROWS_PER_STEP, c)
    cpb = rows // c
    nj = t // rows
    rb0 = row0 // rows
    col = lambda part: (lambda b, h, j: (rb0 + b * nj + j, part * heads + h))
    vec = pl.BlockSpec((1, LANE), lambda b, h, j: (0, h))
    st_spec = pl.BlockSpec((1, 1, LANE, LANE), lambda b, h, j: (b, h, 0, 0))
    o, s_out = pl.pallas_call(
        functools.partial(_hgrn_kernel, c=c, cpb=cpb),
        grid=(nstream, heads, nj),
        in_specs=[pl.BlockSpec((rows, LANE), col(0)), pl.BlockSpec((rows, LANE), col(1)),
                  pl.BlockSpec((rows, LANE), col(2)), pl.BlockSpec((rows, LANE), col(3)),
                  vec, vec, st_spec],
        out_specs=[pl.BlockSpec((rows, LANE), lambda b, h, j: (b * nj + j, h)), st_spec],
        out_shape=[jax.ShapeDtypeStruct((nstream * t, heads * LANE), BF16),
                   jax.ShapeDtypeStruct((nstream, heads, LANE, LANE), F32)],
        scratch_shapes=[pltpu.VMEM((LANE, LANE), F32)],
        compiler_params=_params("parallel", "parallel", "arbitrary"),
        name="hgrn2",
    )(p, p, p, p, lb.reshape(1, -1), norm_g.reshape(1, -1), s0)
    return o, s_out


def _gmlp_kernel(u_ref, v_ref, ng_ref, w_ref, bst_ref, o_ref, *vout_ref, groups):
    v = v_ref[...]
    vn = v * lax.rsqrt(jnp.mean(v * v, axis=-1, keepdims=True) + EPS) * ng_ref[...]
    if vout_ref:
        vout_ref[0][...] = vn
    c = v.shape[0]
    tril = lax.broadcasted_iota(jnp.int32, (c, c), 0) >= lax.broadcasted_iota(jnp.int32, (c, c), 1)
    for g in range(groups):
        cs = slice(g * LANE, (g + 1) * LANE)
        w = jnp.where(tril, w_ref[g], 0.0).astype(BF16)
        s = jnp.dot(w, vn[:, cs].astype(BF16), preferred_element_type=F32) + bst_ref[:, g:g + 1]
        o_ref[:, cs] = (u_ref[:, cs] * s).astype(o_ref.dtype)


def _gmlp(p, row0, nrows, t, col0, d_b, norm_g, w_s, b_s, emit_v):
    groups = w_s.shape[0]
    c = min(t, GMLP_CHUNK)
    w = w_s[:, :c, :c]
    bst = jnp.transpose(b_s[:, :c])
    rb0 = row0 // c
    cb = col0 // d_b
    out_shape = [jax.ShapeDtypeStruct((nrows, d_b), BF16)]
    out_specs = [pl.BlockSpec((c, d_b), lambda i: (i, 0))]
    if emit_v:
        out_shape.append(jax.ShapeDtypeStruct((nrows, d_b), F32))
        out_specs.append(pl.BlockSpec((c, d_b), lambda i: (i, 0)))
    return pl.pallas_call(
        functools.partial(_gmlp_kernel, groups=groups),
        grid=(nrows // c,),
        in_specs=[pl.BlockSpec((c, d_b), lambda i: (rb0 + i, cb)),
                  pl.BlockSpec((c, d_b), lambda i: (rb0 + i, cb + 1)),
                  pl.BlockSpec((1, d_b), lambda i: (0, 0)),
                  pl.BlockSpec((groups, c, c), lambda i: (0, 0, 0)),
                  pl.BlockSpec((c, groups), lambda i: (0, 0))],
        out_specs=out_specs,
        out_shape=out_shape,
        compiler_params=_params("parallel"),
        name="gmlp",
    )(p, p, norm_g.reshape(1, d_b), w, bst)


def _log_sigmoid(z):
    return jnp.minimum(z, 0.0) - jnp.log1p(jnp.exp(-jnp.abs(z)))


def _fox_prep_kernel(cf_ref, bias_ref, logf_ref, f_ref, ft_ref, carry, *, heads):
    @pl.when(pl.program_id(1) == 0)
    def _():
        carry[...] = jnp.zeros_like(carry)

    logf = _log_sigmoid(cf_ref[...] + bias_ref[...])
    logf_ref[...] = logf[:, :heads]
    fc = carry[...] + _row_cumsum(logf)
    carry[...] = fc[fc.shape[0] - 1:, :]
    f_ref[...] = fc
    ft_ref[0] = fc.T[:heads, :]


def _fox_prep(cf, bias, nstream, t, heads):
    tb = _tile(t, 256, LANE)
    nj = t // tb
    return pl.pallas_call(
        functools.partial(_fox_prep_kernel, heads=heads),
        grid=(nstream, nj),
        in_specs=[pl.BlockSpec((tb, LANE), lambda b, j: (b * nj + j, 0)),
                  pl.BlockSpec((1, LANE), lambda b, j: (0, 0))],
        out_specs=[pl.BlockSpec((tb, heads), lambda b, j: (b * nj + j, 0)),
                   pl.BlockSpec((tb, LANE), lambda b, j: (b * nj + j, 0)),
                   pl.BlockSpec((1, heads, tb), lambda b, j: (b, 0, j))],
        out_shape=[jax.ShapeDtypeStruct((nstream * t, heads), F32),
                   jax.ShapeDtypeStruct((nstream * t, LANE), F32),
                   jax.ShapeDtypeStruct((nstream, heads, t), F32)],
        scratch_shapes=[pltpu.VMEM((1, LANE), F32)],
        compiler_params=_params("parallel", "arbitrary"),
        name="fox_prep",
    )(cf, bias)


def _fox_attn_kernel(q_ref, k_ref, v_ref, fq_ref, fk_ref, o_ref,
                     m_scr, l_scr, acc_scr, *, heads, scale, blk):
    i = pl.program_id(1)
    j = pl.program_id(2)

    @pl.when(j == 0)
    def _():
        m_scr[...] = jnp.full_like(m_scr, -jnp.inf)
        l_scr[...] = jnp.zeros_like(l_scr)
        acc_scr[...] = jnp.zeros_like(acc_scr)

    @pl.when(j <= i)
    def _():
        q_pos = i * blk + lax.broadcasted_iota(jnp.int32, (blk, blk), 0)
        k_pos = j * blk + lax.broadcasted_iota(jnp.int32, (blk, blk), 1)
        causal = q_pos >= k_pos
        for h in range(heads):
            cs = slice(h * LANE, (h + 1) * LANE)
            s = lax.dot_general(q_ref[:, cs].astype(BF16), k_ref[:, cs].astype(BF16), NT,
                                preferred_element_type=F32) * scale
            s = s + fq_ref[:, h:h + 1] - fk_ref[0, h:h + 1, :]
            s = jnp.where(causal, s, -jnp.inf)
            m_prev = m_scr[h]
            m_new = jnp.maximum(m_prev, jnp.max(s, axis=-1, keepdims=True))
            alpha = jnp.exp(m_prev - m_new)
            p = jnp.exp(s - m_new)
            l_scr[h] = alpha * l_scr[h] + jnp.sum(p, axis=-1, keepdims=True)
            acc_scr[:, cs] = alpha * acc_scr[:, cs] + jnp.dot(
                p.astype(BF16), v_ref[:, cs].astype(BF16), preferred_element_type=F32)
            m_scr[h] = m_new

    @pl.when(j == i)
    def _():
        for h in range(heads):
            cs = slice(h * LANE, (h + 1) * LANE)
            o_ref[:, cs] = (acc_scr[:, cs] / l_scr[h]).astype(o_ref.dtype)


def _fox_attn_prompt(p, col0, d_c, nstream, t, heads, f_col, f_row):
    blk = _tile(t, ATTN_BLOCK, LANE)
    nb = t // blk
    cb = col0 // d_c
    kv_idx = lambda part: (lambda b, i, j: (b * nb + jnp.minimum(j, i), cb + part))
    return pl.pallas_call(
        functools.partial(_fox_attn_kernel, heads=heads, scale=LANE ** -0.5, blk=blk),
        grid=(nstream, nb, nb),
        in_specs=[pl.BlockSpec((blk, d_c), lambda b, i, j: (b * nb + i, cb)),
                  pl.BlockSpec((blk, d_c), kv_idx(1)),
                  pl.BlockSpec((blk, d_c), kv_idx(2)),
                  pl.BlockSpec((blk, LANE), lambda b, i, j: (b * nb + i, 0)),
                  pl.BlockSpec((1, heads, blk), lambda b, i, j: (b, 0, jnp.minimum(j, i)))],
        out_specs=pl.BlockSpec((blk, d_c), lambda b, i, j: (b * nb + i, 0)),
        out_shape=jax.ShapeDtypeStruct((nstream * t, d_c), BF16),
        scratch_shapes=[pltpu.VMEM((heads, blk, 1), F32),
                        pltpu.VMEM((heads, blk, 1), F32),
                        pltpu.VMEM((blk, d_c), F32)],
        compiler_params=_params("parallel", "parallel", "arbitrary"),
        name="fox_attn_prompt",
    )(p, p, p, f_col, f_row)


def _fox_sample_kernel(q_ref, k_ref, v_ref, cf_ref, bias_ref, pk_ref, pv_ref, plf_ref,
                       o_ref, lf_ref, *, heads, scale):
    ts = q_ref.shape[0]
    past = pk_ref.shape[1]
    f_past = _row_cumsum(plf_ref[0])
    logf = _log_sigmoid(cf_ref[...] + bias_ref[...])
    lf_ref[0] = logf[:, :heads]
    f_new = f_past[past - 1:past, :] + _row_cumsum(logf)
    f_past_t = f_past.T
    f_new_t = jnp.concatenate([f_new, jnp.zeros((LANE - ts, LANE), F32)], axis=0).T
    tril = lax.broadcasted_iota(jnp.int32, (ts, ts), 0) >= lax.broadcasted_iota(jnp.int32, (ts, ts), 1)
    dotf = functools.partial(lax.dot_general, preferred_element_type=F32)
    for h in range(heads):
        cs = slice(h * LANE, (h + 1) * LANE)
        qh = q_ref[:, cs].astype(BF16)
        fq = f_new[:, h:h + 1]
        s_p = dotf(qh, pk_ref[0, :, cs].astype(BF16), NT) * scale + fq - f_past_t[h:h + 1, :]
        s_n = dotf(qh, k_ref[:, cs].astype(BF16), NT) * scale + fq - f_new_t[h:h + 1, :ts]
        s_n = jnp.where(tril, s_n, -jnp.inf)
        m = jnp.maximum(jnp.max(s_p, axis=-1, keepdims=True), jnp.max(s_n, axis=-1, keepdims=True))
        e_p = jnp.exp(s_p - m)
        e_n = jnp.exp(s_n - m)
        l = jnp.sum(e_p, axis=-1, keepdims=True) + jnp.sum(e_n, axis=-1, keepdims=True)
        acc = (jnp.dot(e_p.astype(BF16), pv_ref[0, :, cs].astype(BF16), preferred_element_type=F32)
               + jnp.dot(e_n.astype(BF16), v_ref[:, cs].astype(BF16), preferred_element_type=F32))
        o_ref[:, cs] = (acc / l).astype(o_ref.dtype)


def _fox_attn_sample(p, row0, col0, d_c, cf, bias, past_k, past_v, past_logf, nstream, ts, heads):
    past = past_k.shape[1]
    rb0 = row0 // ts
    cb = col0 // d_c
    new = lambda part: pl.BlockSpec((ts, d_c), lambda b: (rb0 + b, cb + part))
    cache = pl.BlockSpec((1, past, d_c), lambda b: (b, 0, 0))
    return pl.pallas_call(
        functools.partial(_fox_sample_kernel, heads=heads, scale=LANE ** -0.5),
        grid=(nstream,),
        in_specs=[new(0), new(1), new(2),
                  pl.BlockSpec((ts, LANE), lambda b: (rb0 + b, 0)),
                  pl.BlockSpec((1, LANE), lambda b: (0, 0)),
                  cache, cache,
                  pl.BlockSpec((1, past, LANE), lambda b: (b, 0, 0))],
        out_specs=[pl.BlockSpec((ts, d_c), lambda b: (b, 0)),
                   pl.BlockSpec((1, ts, heads), lambda b: (b, 0, 0))],
        out_shape=[jax.ShapeDtypeStruct((nstream * ts, d_c), BF16),
                   jax.ShapeDtypeStruct((nstream, ts, heads), F32)],
        compiler_params=_params("parallel"),
        name="fox_attn_sample",
    )(p, p, p, cf, bias, past_k, past_v, past_logf)


def kernel(x_prompt, x_sample, state_hgrn, cache_k, cache_v, cache_logf, ffn1_norm, ffn1_w_in,
           ffn1_w_out, mix_norm, w_in, w_gate, hgrn_lb_logits, hgrn_norm, gmlp_norm, gmlp_w_s,
           gmlp_b_s, fox_bias, w_branch_a, w_branch_b, w_branch_c, w_out, ffn2_norm, ffn2_w_in,
           ffn2_w_out, final_norm):
    nb, seq, d = x_prompt.shape
    ns, ts, _ = x_sample.shape
    depth = state_hgrn.shape[0]
    a_heads = state_hgrn.shape[2]
    c_heads = cache_k.shape[3]
    past = cache_k.shape[2]
    d_a = a_heads * LANE
    d_b = gmlp_norm.shape[1]
    d_c = c_heads * LANE
    n_main = 4 * d_a + 2 * d_b + 3 * d_c
    col_b = 4 * d_a
    col_c = 4 * d_a + 2 * d_b
    mp = nb * seq
    ms = ns * ts
    assert w_in.shape[2] == n_main + c_heads and c_heads <= LANE
    assert col_b % d_b == 0 and col_c % d_c == 0 and mp % ts == 0

    lower_bounds = _lower_bounds(hgrn_lb_logits)
    x = jnp.concatenate([x_prompt.reshape(mp, d), x_sample.reshape(ms, d)], axis=0)
    s0_prompt = jnp.zeros((nb, a_heads, LANE, LANE), F32)

    outs = {name: [] for name in ("sa_p", "sa_s", "vb_s", "k_p", "k_s", "v_p", "v_s", "lf_p", "lf_s")}
    for l in range(depth):
        w1_in, w1_out = ffn1_w_in[l].astype(BF16), ffn1_w_out[l].astype(BF16)
        w2_in, w2_out = ffn2_w_in[l].astype(BF16), ffn2_w_out[l].astype(BF16)
        w_main = w_in[l, :, :n_main].astype(BF16)
        w_cf = jnp.pad(w_in[l, :, n_main:], ((0, 0), (0, LANE - c_heads))).astype(BF16)
        bias = jnp.pad(fox_bias[l].astype(F32), (0, LANE - c_heads)).reshape(1, LANE)

        hid = _ffn_up(_rmsnorm(x, ffn1_norm[l], BF16), w1_in)
        h = _mm_res(hid, w1_out, x, 0.5, 256)

        n = _rmsnorm(h, mix_norm[l], BF16)
        p = _mm(n, w_main, gate=False)
        g = _mm(n, w_gate[l].astype(BF16), gate=True)
        cf = _mm(n, w_cf, gate=False)

        oa_p, sa_p = _hgrn(p, 0, nb, seq, a_heads, lower_bounds[l], hgrn_norm[l], s0_prompt)
        oa_s, sa_s = _hgrn(p, mp, ns, ts, a_heads, lower_bounds[l], hgrn_norm[l], state_hgrn[l])
        (ob_p,) = _gmlp(p, 0, mp, seq, col_b, d_b, gmlp_norm[l], gmlp_w_s[l], gmlp_b_s[l], False)
        ob_s, vb_s = _gmlp(p, mp, ms, ts, col_b, d_b, gmlp_norm[l], gmlp_w_s[l], gmlp_b_s[l], True)
        lf_p, f_col, f_row = _fox_prep(cf, bias, nb, seq, c_heads)
        oc_p = _fox_attn_prompt(p, col_c, d_c, nb, seq, c_heads, f_col, f_row)
        oc_s, lf_s = _fox_attn_sample(
            p, mp, col_c, d_c, cf, bias,
            cache_k[l].reshape(ns, past, d_c), cache_v[l].reshape(ns, past, d_c),
            jnp.pad(cache_logf[l].astype(F32), ((0, 0), (0, 0), (0, LANE - c_heads))),
            ns, ts, c_heads)

        merged = _branch_merge(
            jnp.concatenate([oa_p, oa_s], axis=0), jnp.concatenate([ob_p, ob_s], axis=0),
            jnp.concatenate([oc_p, oc_s], axis=0),
            w_branch_a[l].astype(BF16), w_branch_b[l].astype(BF16), w_branch_c[l].astype(BF16), g)
        h = _mm_res(merged, w_out[l].astype(BF16), h, 1.0, 1024)

        hid = _ffn_up(_rmsnorm(h, ffn2_norm[l], BF16), w2_in)
        x = _mm_res(hid, w2_out, h, 0.5, 256)

        k_all = p[:, col_c + d_c:col_c + 2 * d_c]
        v_all = p[:, col_c + 2 * d_c:col_c + 3 * d_c]
        outs["sa_p"].append(sa_p)
        outs["sa_s"].append(sa_s)
        outs["vb_s"].append(vb_s.reshape(ns, ts, d_b))
        outs["k_p"].append(k_all[:mp].reshape(nb, seq, c_heads, LANE))
        outs["k_s"].append(k_all[mp:].reshape(ns, ts, c_heads, LANE))
        outs["v_p"].append(v_all[:mp].reshape(nb, seq, c_heads, LANE))
        outs["v_s"].append(v_all[mp:].reshape(ns, ts, c_heads, LANE))
        outs["lf_p"].append(lf_p.reshape(nb, seq, c_heads))
        outs["lf_s"].append(lf_s)

    y = _rmsnorm(x, final_norm, F32)
    stack = lambda name: jnp.stack(outs[name])
    return (y[:mp].reshape(nb, seq, d), y[mp:].reshape(ns, ts, d),
            stack("sa_p"), stack("sa_s"), stack("vb_s"),
            stack("k_p"), stack("k_s"), stack("v_p"), stack("v_s"),
            stack("lf_p"), stack("lf_s"))
```

```python
import functools
import math

import jax
import jax.numpy as jnp
from jax import lax
from jax.experimental import pallas as pl
from jax.experimental.pallas import tpu as pltpu

EPS = 1e-6
LOG2E = 1.4426950408889634
LANE = 128
HGRN_CHUNK = 64
HGRN_ROWS_PER_STEP = 256
HGRN_HEADS_PER_STEP = 8
GMLP_CHUNK = 128
ATTN_BLOCK = 256
ATTN_KEY_SUB = 128
VMEM_LIMIT_BYTES = 56 * 1024 * 1024

F32 = jnp.float32
BF16 = jnp.bfloat16
NT = (((1,), (1,)), ((), ()))
TN = (((0,), (0,)), ((), ()))
ANY_SPEC = pl.BlockSpec(memory_space=pl.ANY)


def _tile(n, target, align):
    t = (min(target, n) // align) * align
    while t >= align:
        if n % t == 0:
            return t
        t -= align
    return n


def _params(*sem):
    return pltpu.CompilerParams(dimension_semantics=sem, vmem_limit_bytes=VMEM_LIMIT_BYTES)


def _sigmoid(x):
    return jax.nn.sigmoid(x)


def _row_cumsum(x):
    n = x.shape[0]
    row = lax.broadcasted_iota(jnp.int32, x.shape, 0)
    sh = 1
    while sh < n:
        x = x + jnp.where(row >= sh, pltpu.roll(x, sh, 0), 0.0)
        sh *= 2
    return x


def _without_refs(kernel, start, count):
    if count == 0:
        return kernel

    def wrapped(*refs):
        return kernel(*refs[:start], *refs[start + count:])
    return wrapped


def _rmsnorm_kernel(x_ref, g_ref, o_ref):
    x = x_ref[...]
    y = x * lax.rsqrt(jnp.mean(x * x, axis=-1, keepdims=True) + EPS)
    o_ref[...] = (y * g_ref[...]).astype(o_ref.dtype)


def _rmsnorm(x, g, out_dtype, row0=0, nrows=None):
    d = x.shape[1]
    nrows = x.shape[0] if nrows is None else nrows
    tm = _tile(math.gcd(row0, nrows), 256, 16)
    rb0 = row0 // tm
    return pl.pallas_call(
        _rmsnorm_kernel,
        grid=(nrows // tm,),
        in_specs=[pl.BlockSpec((tm, d), lambda i: (rb0 + i, 0)),
                  pl.BlockSpec((1, d), lambda i: (0, 0))],
        out_specs=pl.BlockSpec((tm, d), lambda i: (i, 0)),
        out_shape=jax.ShapeDtypeStruct((nrows, d), out_dtype),
        compiler_params=_params("parallel"),
        name="rmsnorm",
    )(x, g.reshape(1, d))


def _ffn_up_kernel(a_ref, wg_ref, wu_ref, o_ref):
    a = a_ref[...]
    gate = jnp.dot(a, wg_ref[...].astype(BF16), preferred_element_type=F32)
    up = jnp.dot(a, wu_ref[...].astype(BF16), preferred_element_type=F32)
    o_ref[...] = (gate * _sigmoid(gate) * up).astype(o_ref.dtype)


def _ffn_up(a, w_in, l):
    m, d = a.shape
    f = w_in.shape[2] // 2
    tm = _tile(m, 1536, 16)
    tn = _tile(f, 256, LANE)
    nf = f // tn
    return pl.pallas_call(
        _ffn_up_kernel,
        grid=(m // tm, nf),
        in_specs=[pl.BlockSpec((tm, d), lambda i, j: (i, 0)),
                  pl.BlockSpec((None, d, tn), lambda i, j: (l, 0, j)),
                  pl.BlockSpec((None, d, tn), lambda i, j: (l, 0, j + nf))],
        out_specs=pl.BlockSpec((tm, tn), lambda i, j: (i, j)),
        out_shape=jax.ShapeDtypeStruct((m, f), BF16),
        compiler_params=_params("parallel", "arbitrary"),
        name="ffn_up",
    )(a, w_in, w_in)


def _mm_res_kernel(a_ref, w_ref, r_ref, o_ref, *, scale):
    acc = jnp.dot(a_ref[...], w_ref[...], preferred_element_type=F32)
    o_ref[...] = r_ref[...] + scale * acc


def _mm_res(a, w, l, r, scale, tn_target):
    m, k = a.shape
    n = w.shape[2]
    tm = _tile(m, 768, 16)
    tn = _tile(n, tn_target, LANE)
    return pl.pallas_call(
        functools.partial(_mm_res_kernel, scale=scale),
        grid=(m // tm, n // tn),
        in_specs=[pl.BlockSpec((tm, k), lambda i, j: (i, 0)),
                  pl.BlockSpec((None, k, tn), lambda i, j: (l, 0, j)),
                  pl.BlockSpec((tm, tn), lambda i, j: (i, j))],
        out_specs=pl.BlockSpec((tm, tn), lambda i, j: (i, j)),
        out_shape=jax.ShapeDtypeStruct((m, n), F32),
        compiler_params=_params("parallel", "arbitrary"),
        name="mm_res",
    )(a, w, r)


def _mm_kernel(a_ref, w_ref, o_ref, *, gate):
    acc = jnp.dot(a_ref[...], w_ref[...], preferred_element_type=F32)
    o_ref[...] = (_sigmoid(acc) if gate else acc).astype(o_ref.dtype)


def _mm(a, w, l, col0, ncols, out_dtype, gate=False, tn_target=1024):
    m, k = a.shape
    tm = _tile(m, 768, 16)
    tn = _tile(math.gcd(col0, ncols), tn_target, LANE)
    cb0 = col0 // tn
    return pl.pallas_call(
        functools.partial(_mm_kernel, gate=gate),
        grid=(m // tm, ncols // tn),
        in_specs=[pl.BlockSpec((tm, k), lambda i, j: (i, 0)),
                  pl.BlockSpec((None, k, tn), lambda i, j: (l, 0, cb0 + j))],
        out_specs=pl.BlockSpec((tm, tn), lambda i, j: (i, j)),
        out_shape=jax.ShapeDtypeStruct((m, ncols), out_dtype),
        compiler_params=_params("parallel", "arbitrary"),
        name="mm_gate" if gate else "mm",
    )(a, w)


def _kv_kernel(a_ref, wk_ref, wv_ref, kf_ref, vf_ref, kb_ref, vb_ref, *, transpose_v):
    a = a_ref[...]
    k = jnp.dot(a, wk_ref[...], preferred_element_type=F32)
    v = jnp.dot(a, wv_ref[...], preferred_element_type=F32)
    kf_ref[...] = k
    vf_ref[...] = v
    kb_ref[...] = k.astype(BF16)
    vb_ref[...] = (v.T if transpose_v else v).astype(BF16)


def _kv_proj(a, w, l, col_k, d_c, row0, nrows, depth, prev, transpose_v):
    d = a.shape[1]
    tm = _tile(math.gcd(row0, nrows), 512, 16)
    tn = _tile(d_c, 512, LANE)
    rb0 = row0 // tm
    ck = col_k // tn
    cv = (col_k + d_c) // tn
    stacked = jax.ShapeDtypeStruct((depth, nrows, d_c), F32)
    layer = jax.ShapeDtypeStruct((nrows, d_c), BF16)
    n_prev = 0 if prev is None else 2
    stack_spec = pl.BlockSpec((None, tm, tn), lambda i, j: (l, i, j))
    layer_spec = pl.BlockSpec((tm, tn), lambda i, j: (i, j))
    if transpose_v:
        v_shape = jax.ShapeDtypeStruct((d_c, nrows), BF16)
        v_spec = pl.BlockSpec((tn, tm), lambda i, j: (j, i))
    else:
        v_shape, v_spec = layer, layer_spec
    return pl.pallas_call(
        _without_refs(functools.partial(_kv_kernel, transpose_v=transpose_v), 3, n_prev),
        grid=(nrows // tm, d_c // tn),
        in_specs=[pl.BlockSpec((tm, d), lambda i, j: (rb0 + i, 0)),
                  pl.BlockSpec((None, d, tn), lambda i, j: (l, 0, ck + j)),
                  pl.BlockSpec((None, d, tn), lambda i, j: (l, 0, cv + j))] + [ANY_SPEC] * n_prev,
        out_specs=[stack_spec, stack_spec, layer_spec, v_spec],
        out_shape=[stacked, stacked, layer, v_shape],
        input_output_aliases={3: 0, 4: 1} if prev is not None else {},
        compiler_params=_params("parallel", "arbitrary"),
        name="kv_proj",
    )(a, w, w, *(prev or ()))


def _branch_kernel(oa_ref, ob_ref, oc_ref, wa_ref, wb_ref, wc_ref,
                   ga_ref, gb_ref, gc_ref, o_ref):
    dot = functools.partial(jnp.dot, preferred_element_type=F32)
    merged = (ga_ref[...] * dot(oa_ref[...], wa_ref[...])
              + gb_ref[...] * dot(ob_ref[...], wb_ref[...])
              + gc_ref[...] * dot(oc_ref[...], wc_ref[...]))
    o_ref[...] = merged.astype(o_ref.dtype)


def _branch_merge(oa, ob, oc, wa, wb, wc, l, g):
    m = oa.shape[0]
    d = wa.shape[2]
    tm = _tile(m, 768, 16)
    tn = _tile(d, 512, LANE)
    nd = d // tn
    a_spec = lambda x: pl.BlockSpec((tm, x.shape[1]), lambda i, j: (i, 0))
    w_spec = lambda x: pl.BlockSpec((None, x.shape[1], tn), lambda i, j: (l, 0, j))
    g_spec = lambda b: pl.BlockSpec((tm, tn), lambda i, j: (i, j + b * nd))
    return pl.pallas_call(
        _branch_kernel,
        grid=(m // tm, nd),
        in_specs=[a_spec(oa), a_spec(ob), a_spec(oc), w_spec(wa), w_spec(wb), w_spec(wc),
                  g_spec(0), g_spec(1), g_spec(2)],
        out_specs=pl.BlockSpec((tm, tn), lambda i, j: (i, j)),
        out_shape=jax.ShapeDtypeStruct((m, d), BF16),
        compiler_params=_params("parallel", "arbitrary"),
        name="branch_merge",
    )(oa, ob, oc, wa, wb, wc, g, g, g)


def _lower_bound_kernel(x_ref, o_ref):
    x = x_ref[...]
    depth = x.shape[0]
    e = jnp.exp(x - jnp.max(x, axis=0, keepdims=True))
    w = e / jnp.sum(e, axis=0, keepdims=True)
    run = w[0:1, :]
    rows = [run - w[0:1, :]]
    for l in range(1, depth):
        run = run + w[l:l + 1, :]
        rows.append(run - w[0:1, :])
    o_ref[...] = jnp.concatenate(rows, axis=0)


def _lower_bounds(logits):
    return pl.pallas_call(
        _lower_bound_kernel,
        out_shape=jax.ShapeDtypeStruct(logits.shape, F32),
        name="hgrn_lower_bounds",
    )(logits.astype(F32))


def _hgrn_kernel(q_ref, f_ref, i_ref, g_ref, lb_ref, ng_ref, s0_ref,
                 o_ref, sout_ref, st_scr, *, c, cpb, hpb):
    j = pl.program_id(2)

    @pl.when(j == 0)
    def _():
        for hh in range(hpb):
            st_scr[hh] = s0_ref[0, hh].T

    rowi = lax.broadcasted_iota(jnp.int32, (c, LANE), 0)
    ti = lax.broadcasted_iota(jnp.int32, (c, c), 0)
    si = lax.broadcasted_iota(jnp.int32, (c, c), 1)
    tx = ti ^ si
    eye = ti == si
    levels = []
    hb = 1
    while hb < c:
        levels.append(hb)
        hb *= 2
    masks = [(ti > si) & (tx >= hb) & (tx < 2 * hb) for hb in levels]
    dotf = functools.partial(lax.dot_general, preferred_element_type=F32)

    def level_ref(b, hb):
        n = 2 * hb
        if n == 2:
            return jnp.where((rowi & 1) != 0, pltpu.roll(b, 1, 0), b)
        if n == 4:
            m = rowi & 3
            return jnp.where(m == 0, pltpu.roll(b, c - 1, 0),
                             jnp.where(m == 1, b,
                                       jnp.where(m == 2, pltpu.roll(b, 1, 0),
                                                 pltpu.roll(b, 2, 0))))
        pieces = [jnp.broadcast_to(b[base + hb - 1:base + hb, :], (n, LANE))
                  for base in range(0, c, n)]
        return pieces[0] if len(pieces) == 1 else jnp.concatenate(pieces, axis=0)

    def head_chunk(rows, hh):
        cs = slice(hh * LANE, (hh + 1) * LANE)
        lb = lb_ref[:, cs]
        f = lb + (1.0 - lb) * _sigmoid(f_ref[rows, cs])
        kk = 1.0 - f
        qr = q_ref[rows, cs]
        q = qr * _sigmoid(qr)
        iv = i_ref[rows, cs].astype(BF16)
        b = _row_cumsum(jnp.log(f) * LOG2E)
        b_last = b[c - 1:c, :]
        st = st_scr[hh]
        o = dotf((q * jnp.exp2(b)).astype(BF16), st.astype(BF16), NT)
        a = jnp.where(eye, dotf(q.astype(BF16), kk.astype(BF16), NT), 0.0)
        for hb, mask in zip(levels, masks):
            upper = (rowi & hb) != 0
            d = b - level_ref(b, hb)
            x = (jnp.where(upper, q, kk) * jnp.exp2(jnp.where(upper, d, -d))).astype(BF16)
            a = a + jnp.where(mask, dotf(x, x, NT), 0.0)
        o = o + jnp.dot(a.astype(BF16), iv, preferred_element_type=F32)
        kb = kk * jnp.exp2(b_last - b)
        st_scr[hh] = st * jnp.exp2(b_last) + dotf(iv, kb.astype(BF16), TN)
        o = o * lax.rsqrt(jnp.mean(o * o, axis=-1, keepdims=True) + EPS) * ng_ref[:, cs]
        gr = g_ref[rows, cs]
        o_ref[rows, cs] = (o * (gr * _sigmoid(gr))).astype(o_ref.dtype)

    def chunk(ci, carry):
        rows = pl.ds(pl.multiple_of(ci * c, c), c)
        for hh in range(hpb):
            head_chunk(rows, hh)
        return carry

    lax.fori_loop(0, cpb, chunk, 0)

    @pl.when(j == pl.num_programs(2) - 1)
    def _():
        for hh in range(hpb):
            sout_ref[0, hh] = st_scr[hh].T


def _hgrn(p, row0, nstream, t, heads, lb, norm_g, s0, prev):
    m = p.shape[0]
    c = min(t, HGRN_CHUNK)
    rows = _tile(t, HGRN_ROWS_PER_STEP, c)
    cpb = rows // c
    nj = t // rows
    rb0 = row0 // rows
    hpb = _tile(heads, HGRN_HEADS_PER_STEP, 1)
    ng = heads // hpb
    w = hpb * LANE
    col = lambda part: (lambda b, hg, j: (rb0 + b * nj + j, part * ng + hg))
    vec = pl.BlockSpec((1, w), lambda b, hg, j: (0, hg))
    st_spec = pl.BlockSpec((1, hpb, LANE, LANE), lambda b, hg, j: (b, hg, 0, 0))
    n_prev = 0 if prev is None else 1
    o, s_out = pl.pallas_call(
        _without_refs(functools.partial(_hgrn_kernel, c=c, cpb=cpb, hpb=hpb), 7, n_prev),
        grid=(nstream, ng, nj),
        in_specs=[pl.BlockSpec((rows, w), col(0)), pl.BlockSpec((rows, w), col(1)),
                  pl.BlockSpec((rows, w), col(2)), pl.BlockSpec((rows, w), col(3)),
                  vec, vec, st_spec] + [ANY_SPEC] * n_prev,
        out_specs=[pl.BlockSpec((rows, w), lambda b, hg, j: (rb0 + b * nj + j, hg)), st_spec],
        out_shape=[jax.ShapeDtypeStruct((m, heads * LANE), BF16),
                   jax.ShapeDtypeStruct((nstream, heads, LANE, LANE), F32)],
        scratch_shapes=[pltpu.VMEM((hpb, LANE, LANE), F32)],
        input_output_aliases={7: 0} if prev is not None else {},
        compiler_params=_params("parallel", "parallel", "arbitrary"),
        name="hgrn2",
    )(p, p, p, p, lb.reshape(1, -1), norm_g.reshape(1, -1), s0, *(() if prev is None else (prev,)))
    return o, s_out


def _gmlp_kernel(u_ref, v_ref, ng_ref, w_ref, bst_ref, o_ref, *vout_ref, groups):
    v = v_ref[...]
    vn = v * lax.rsqrt(jnp.mean(v * v, axis=-1, keepdims=True) + EPS) * ng_ref[...]
    if vout_ref:
        vout_ref[0][...] = vn
    c = v.shape[0]
    tril = lax.broadcasted_iota(jnp.int32, (c, c), 0) >= lax.broadcasted_iota(jnp.int32, (c, c), 1)
    for g in range(groups):
        cs = slice(g * LANE, (g + 1) * LANE)
        w = jnp.where(tril, w_ref[g], 0.0).astype(BF16)
        s = jnp.dot(w, vn[:, cs].astype(BF16), preferred_element_type=F32) + bst_ref[:, g:g + 1]
        o_ref[:, cs] = (u_ref[:, cs] * s).astype(o_ref.dtype)


def _gmlp(p, row0, nrows, t, col0, d_b, norm_g, w_s, b_s, prev):
    m = p.shape[0]
    groups = w_s.shape[0]
    c = min(t, GMLP_CHUNK)
    w = w_s[:, :c, :c]
    bst = jnp.transpose(b_s[:, :c])
    rb0 = row0 // c
    cb = col0 // d_b
    emit_v = prev is not None
    out_shape = [jax.ShapeDtypeStruct((m, d_b), BF16)]
    out_specs = [pl.BlockSpec((c, d_b), lambda i: (rb0 + i, 0))]
    if emit_v:
        out_shape.append(jax.ShapeDtypeStruct((nrows, d_b), F32))
        out_specs.append(pl.BlockSpec((c, d_b), lambda i: (i, 0)))
    n_prev = 0 if prev is None else 1
    return pl.pallas_call(
        _without_refs(functools.partial(_gmlp_kernel, groups=groups), 5, n_prev),
        grid=(nrows // c,),
        in_specs=[pl.BlockSpec((c, d_b), lambda i: (rb0 + i, cb)),
                  pl.BlockSpec((c, d_b), lambda i: (rb0 + i, cb + 1)),
                  pl.BlockSpec((1, d_b), lambda i: (0, 0)),
                  pl.BlockSpec((groups, c, c), lambda i: (0, 0, 0)),
                  pl.BlockSpec((c, groups), lambda i: (0, 0))] + [ANY_SPEC] * n_prev,
        out_specs=out_specs,
        out_shape=out_shape,
        input_output_aliases={5: 0} if prev is not None else {},
        compiler_params=_params("parallel"),
        name="gmlp",
    )(p, p, norm_g.reshape(1, d_b), w, bst, *(() if prev is None else (prev,)))


def _log_sigmoid(z):
    return jnp.minimum(z, 0.0) - jnp.log1p(jnp.exp(-jnp.abs(z)))


def _fox_prep_kernel(cf_ref, bias_ref, logf_ref, f_ref, ft_ref, carry, *, heads):
    @pl.when(pl.program_id(1) == 0)
    def _():
        carry[...] = jnp.zeros_like(carry)

    logf = _log_sigmoid(cf_ref[...] + bias_ref[...])
    logf_ref[...] = logf[:, :heads]
    fc = carry[...] + _row_cumsum(logf)
    carry[...] = fc[fc.shape[0] - 1:, :]
    fc = fc * LOG2E
    f_ref[...] = fc
    ft_ref[0] = fc.T[:heads, :]


def _fox_prep(cf, bias, nstream, t, heads):
    tb = _tile(t, 256, LANE)
    nj = t // tb
    return pl.pallas_call(
        functools.partial(_fox_prep_kernel, heads=heads),
        grid=(nstream, nj),
        in_specs=[pl.BlockSpec((tb, LANE), lambda b, j: (b * nj + j, 0)),
                  pl.BlockSpec((1, LANE), lambda b, j: (0, 0))],
        out_specs=[pl.BlockSpec((tb, heads), lambda b, j: (b * nj + j, 0)),
                   pl.BlockSpec((tb, LANE), lambda b, j: (b * nj + j, 0)),
                   pl.BlockSpec((1, heads, tb), lambda b, j: (b, 0, j))],
        out_shape=[jax.ShapeDtypeStruct((nstream * t, heads), F32),
                   jax.ShapeDtypeStruct((nstream * t, LANE), F32),
                   jax.ShapeDtypeStruct((nstream, heads, t), F32)],
        scratch_shapes=[pltpu.VMEM((1, LANE), F32)],
        compiler_params=_params("parallel", "arbitrary"),
        name="fox_prep",
    )(cf, bias)


def _fox_attn_kernel(q_ref, k_ref, vt_ref, fq_ref, fk_ref, o_ref,
                     m_scr, l_scr, acc_scr, *, heads, scale, blk, ks):
    i = pl.program_id(1)
    j = pl.program_id(2)
    nsub = blk // ks

    @pl.when(j == 0)
    def _():
        m_scr[...] = jnp.full_like(m_scr, -jnp.inf)
        l_scr[...] = jnp.zeros_like(l_scr)
        acc_scr[...] = jnp.zeros_like(acc_scr)

    def step(diagonal):
        if diagonal:
            key = lax.broadcasted_iota(jnp.int32, (ks, blk), 0)
            qry = lax.broadcasted_iota(jnp.int32, (ks, blk), 1)
            hide = [jnp.where(qry >= key + u * ks, 0.0, -jnp.inf) for u in range(nsub)]
        for h in range(heads):
            cs = slice(h * LANE, (h + 1) * LANE)
            fq = fq_ref[0, h:h + 1, :]
            q = q_ref[:, cs]
            for u in range(nsub):
                rs = slice(u * ks, (u + 1) * ks)
                z = lax.dot_general(k_ref[rs, cs], q, NT,
                                    preferred_element_type=F32) * scale - fk_ref[rs, h:h + 1]
                if diagonal:
                    z = z + hide[u]
                m_prev = m_scr[h]
                m_new = jnp.maximum(m_prev, jnp.max(z, axis=0, keepdims=True) + fq)
                alpha = jnp.exp2(m_prev - m_new)
                p = jnp.exp2(z + (fq - m_new))
                l_scr[h] = alpha * l_scr[h] + jnp.sum(p, axis=0, keepdims=True)
                acc_scr[h] = alpha * acc_scr[h] + jnp.dot(
                    vt_ref[cs, rs], p.astype(BF16), preferred_element_type=F32)
                m_scr[h] = m_new

    @pl.when(j < i)
    def _():
        step(False)

    @pl.when(j == i)
    def _():
        step(True)
        for h in range(heads):
            cs = slice(h * LANE, (h + 1) * LANE)
            o_ref[:, cs] = (acc_scr[h] / l_scr[h]).T.astype(o_ref.dtype)


def _fox_attn_prompt(q, k, vt, m_total, nstream, t, heads, f_col, f_row):
    d_c = heads * LANE
    blk = _tile(t, ATTN_BLOCK, LANE)
    ks = _tile(blk, ATTN_KEY_SUB, LANE)
    nb = t // blk
    kv_row = lambda b, i, j: b * nb + jnp.minimum(j, i)
    stat = pltpu.VMEM((heads, 1, blk), F32)
    return pl.pallas_call(
        functools.partial(_fox_attn_kernel, heads=heads, scale=LANE ** -0.5 * LOG2E, blk=blk, ks=ks),
        grid=(nstream, nb, nb),
        in_specs=[pl.BlockSpec((blk, d_c), lambda b, i, j: (b * nb + i, 0)),
                  pl.BlockSpec((blk, d_c), lambda b, i, j: (kv_row(b, i, j), 0)),
                  pl.BlockSpec((d_c, blk), lambda b, i, j: (0, kv_row(b, i, j))),
                  pl.BlockSpec((1, heads, blk), lambda b, i, j: (b, 0, i)),
                  pl.BlockSpec((blk, LANE), lambda b, i, j: (kv_row(b, i, j), 0))],
        out_specs=pl.BlockSpec((blk, d_c), lambda b, i, j: (b * nb + i, 0)),
        out_shape=jax.ShapeDtypeStruct((m_total, d_c), BF16),
        scratch_shapes=[stat, stat, pltpu.VMEM((heads, LANE, blk), F32)],
        compiler_params=_params("parallel", "parallel", "arbitrary"),
        name="fox_attn_prompt",
    )(q, k, vt, f_row, f_col)


def _fox_sample_kernel(q_ref, k_ref, v_ref, cf_ref, bias_ref, pk_ref, pv_ref, plf_ref,
                       o_ref, lf_ref, *, heads, scale):
    ts = q_ref.shape[0]
    past = pk_ref.shape[1]
    f_past = _row_cumsum(plf_ref[0])
    logf = _log_sigmoid(cf_ref[...] + bias_ref[...])
    lf_ref[0] = logf[:, :heads]
    f_new = f_past[past - 1:past, :] + _row_cumsum(logf)
    f_past_t = f_past.T
    f_new_t = jnp.concatenate([f_new, jnp.zeros((LANE - ts, LANE), F32)], axis=0).T
    tril = lax.broadcasted_iota(jnp.int32, (ts, ts), 0) >= lax.broadcasted_iota(jnp.int32, (ts, ts), 1)
    dotf = functools.partial(lax.dot_general, preferred_element_type=F32)
    for h in range(heads):
        cs = slice(h * LANE, (h + 1) * LANE)
        qh = q_ref[:, cs]
        fq = f_new[:, h:h + 1]
        s_p = dotf(qh, pk_ref[0, :, cs].astype(BF16), NT) * scale + fq - f_past_t[h:h + 1, :]
        s_n = dotf(qh, k_ref[:, cs], NT) * scale + fq - f_new_t[h:h + 1, :ts]
        s_n = jnp.where(tril, s_n, -jnp.inf)
        m = jnp.maximum(jnp.max(s_p, axis=-1, keepdims=True), jnp.max(s_n, axis=-1, keepdims=True))
        e_p = jnp.exp(s_p - m)
        e_n = jnp.exp(s_n - m)
        l = jnp.sum(e_p, axis=-1, keepdims=True) + jnp.sum(e_n, axis=-1, keepdims=True)
        acc = (jnp.dot(e_p.astype(BF16), pv_ref[0, :, cs].astype(BF16), preferred_element_type=F32)
               + jnp.dot(e_n.astype(BF16), v_ref[:, cs], preferred_element_type=F32))
        o_ref[:, cs] = (acc / l).astype(o_ref.dtype)


def _fox_attn_sample(q, row0, k_new, v_new, cf, bias, past_k, past_v, past_logf, l,
                     nstream, ts, heads, prev):
    d_c = heads * LANE
    past = past_k.shape[2]
    rb0 = row0 // ts
    new = pl.BlockSpec((ts, d_c), lambda b: (b, 0))
    cache = pl.BlockSpec((None, 1, past, d_c), lambda b: (l, b, 0, 0))
    return pl.pallas_call(
        _without_refs(functools.partial(_fox_sample_kernel, heads=heads, scale=LANE ** -0.5), 8, 1),
        grid=(nstream,),
        in_specs=[pl.BlockSpec((ts, d_c), lambda b: (rb0 + b, 0)), new, new,
                  pl.BlockSpec((ts, LANE), lambda b: (rb0 + b, 0)),
                  pl.BlockSpec((1, LANE), lambda b: (0, 0)),
                  cache, cache,
                  pl.BlockSpec((None, 1, past, LANE), lambda b: (l, b, 0, 0)),
                  ANY_SPEC],
        out_specs=[pl.BlockSpec((ts, d_c), lambda b: (rb0 + b, 0)),
                   pl.BlockSpec((1, ts, heads), lambda b: (b, 0, 0))],
        out_shape=[jax.ShapeDtypeStruct(prev.shape, BF16),
                   jax.ShapeDtypeStruct((nstream, ts, heads), F32)],
        input_output_aliases={8: 0},
        compiler_params=_params("parallel"),
        name="fox_attn_sample",
    )(q, k_new, v_new, cf, bias, past_k, past_v, past_logf, prev)


def kernel(x_prompt, x_sample, state_hgrn, cache_k, cache_v, cache_logf, ffn1_norm, ffn1_w_in,
           ffn1_w_out, mix_norm, w_in, w_gate, hgrn_lb_logits, hgrn_norm, gmlp_norm, gmlp_w_s,
           gmlp_b_s, fox_bias, w_branch_a, w_branch_b, w_branch_c, w_out, ffn2_norm, ffn2_w_in,
           ffn2_w_out, final_norm):
    nb, seq, d = x_prompt.shape
    ns, ts, _ = x_sample.shape
    depth = state_hgrn.shape[0]
    a_heads = state_hgrn.shape[2]
    c_heads = cache_k.shape[3]
    past = cache_k.shape[2]
    d_a = a_heads * LANE
    d_b = gmlp_norm.shape[1]
    d_c = c_heads * LANE
    col_b = 4 * d_a
    col_q = col_b + 2 * d_b
    col_k = col_q + d_c
    n_main = col_k + 2 * d_c
    mp = nb * seq
    ms = ns * ts
    assert w_in.shape[2] == n_main + c_heads and c_heads <= LANE
    assert col_b % d_b == 0 and mp % ts == 0

    bf = lambda w: w.astype(BF16)
    w1_in, w2_in = ffn1_w_in, ffn2_w_in
    w1_out, w2_out = bf(ffn1_w_out), bf(ffn2_w_out)
    w_in_b, w_gate_b, w_out_b = bf(w_in), bf(w_gate), bf(w_out)
    wa, wb, wc = bf(w_branch_a), bf(w_branch_b), bf(w_branch_c)
    w_cf = bf(jnp.pad(w_in[:, :, n_main:], ((0, 0), (0, 0), (0, LANE - c_heads))))
    bias_all = jnp.pad(fox_bias.astype(F32), ((0, 0), (0, LANE - c_heads)))
    past_k = cache_k.reshape(depth, ns, past, d_c)
    past_v = cache_v.reshape(depth, ns, past, d_c)
    past_lf = jnp.pad(cache_logf.astype(F32), ((0, 0), (0, 0), (0, 0), (0, LANE - c_heads)))

    lower_bounds = _lower_bounds(hgrn_lb_logits)
    x = jnp.concatenate([x_prompt.reshape(mp, d), x_sample.reshape(ms, d)], axis=0)
    s0_prompt = jnp.zeros((nb, a_heads, LANE, LANE), F32)

    kv_p = kv_s = None
    outs = {name: [] for name in ("sa_p", "sa_s", "vb_s", "lf_p", "lf_s")}
    for l in range(depth):
        bias = bias_all[l:l + 1]

        hid = _ffn_up(_rmsnorm(x, ffn1_norm[l], BF16), w1_in, l)
        h = _mm_res(hid, w1_out, l, x, 0.5, 256)

        n = _rmsnorm(h, mix_norm[l], BF16)
        p_ab = _mm(n, w_in_b, l, 0, col_q, F32)
        q_c = _mm(n, w_in_b, l, col_q, d_c, BF16)
        g = _mm(n, w_gate_b, l, 0, 3 * d, F32, gate=True)
        cf = _mm(n, w_cf, l, 0, LANE, F32)
        kp_all, vp_all, kp, vpt = _kv_proj(n, w_in_b, l, col_k, d_c, 0, mp, depth, kv_p, True)
        ks_all, vs_all, ks, vs = _kv_proj(n, w_in_b, l, col_k, d_c, mp, ms, depth, kv_s, False)
        kv_p, kv_s = (kp_all, vp_all), (ks_all, vs_all)

        oa, sa_p = _hgrn(p_ab, 0, nb, seq, a_heads, lower_bounds[l], hgrn_norm[l], s0_prompt, None)
        oa, sa_s = _hgrn(p_ab, mp, ns, ts, a_heads, lower_bounds[l], hgrn_norm[l], state_hgrn[l], oa)
        (ob,) = _gmlp(p_ab, 0, mp, seq, col_b, d_b, gmlp_norm[l], gmlp_w_s[l], gmlp_b_s[l], None)
        ob, vb_s = _gmlp(p_ab, mp, ms, ts, col_b, d_b, gmlp_norm[l], gmlp_w_s[l], gmlp_b_s[l], ob)
        lf_p, f_col, f_row = _fox_prep(cf, bias, nb, seq, c_heads)
        oc = _fox_attn_prompt(q_c, kp, vpt, mp + ms, nb, seq, c_heads, f_col, f_row)
        oc, lf_s = _fox_attn_sample(q_c, mp, ks, vs, cf, bias, past_k, past_v, past_lf, l,
                                    ns, ts, c_heads, oc)

        merged = _branch_merge(oa, ob, oc, wa, wb, wc, l, g)
        h = _mm_res(merged, w_out_b, l, h, 1.0, 1024)

        hid = _ffn_up(_rmsnorm(h, ffn2_norm[l], BF16), w2_in, l)
        x = _mm_res(hid, w2_out, l, h, 0.5, 256)

        outs["sa_p"].append(sa_p)
        outs["sa_s"].append(sa_s)
        outs["vb_s"].append(vb_s.reshape(ns, ts, d_b))
        outs["lf_p"].append(lf_p.reshape(nb, seq, c_heads))
        outs["lf_s"].append(lf_s)

    y_p = _rmsnorm(x, final_norm, F32, 0, mp)
    y_s = _rmsnorm(x, final_norm, F32, mp, ms)
    stack = lambda name: jnp.stack(outs[name])
    return (y_p.reshape(nb, seq, d), y_s.reshape(ns, ts, d),
            stack("sa_p"), stack("sa_s"), stack("vb_s"),
            kv_p[0].reshape(depth, nb, seq, c_heads, LANE), kv_s[0].reshape(depth, ns, ts, c_heads, LANE),
            kv_p[1].reshape(depth, nb, seq, c_heads, LANE), kv_s[1].reshape(depth, ns, ts, c_heads, LANE),
            stack("lf_p"), stack("lf_s"))
```

```python
import functools
import math

import jax
import jax.numpy as jnp
from jax import lax
from jax.experimental import pallas as pl
from jax.experimental.pallas import tpu as pltpu

EPS = 1e-6
LOG2E = 1.4426950408889634
LANE = 128
HGRN_CHUNK = 64
HGRN_ROWS_PER_STEP = 256
HGRN_HEADS_PER_STEP = 8
GMLP_CHUNK = 128
ATTN_BLOCK = 256
ATTN_KEY_SUB = 128
VMEM_LIMIT_BYTES = 56 * 1024 * 1024

F32 = jnp.float32
BF16 = jnp.bfloat16
NT = (((1,), (1,)), ((), ()))
TN = (((0,), (0,)), ((), ()))
ANY_SPEC = pl.BlockSpec(memory_space=pl.ANY)


def _tile(n, target, align):
    t = (min(target, n) // align) * align
    while t >= align:
        if n % t == 0:
            return t
        t -= align
    return n


def _tile_major(w, tn):
    depth, k, n = w.shape
    return w.reshape(depth, k, n // tn, tn).transpose(0, 2, 1, 3)


def _params(*sem):
    return pltpu.CompilerParams(dimension_semantics=sem, vmem_limit_bytes=VMEM_LIMIT_BYTES)


def _sigmoid(x):
    return jax.nn.sigmoid(x)


def _row_cumsum(x):
    n = x.shape[0]
    row = lax.broadcasted_iota(jnp.int32, x.shape, 0)
    sh = 1
    while sh < n:
        x = x + jnp.where(row >= sh, pltpu.roll(x, sh, 0), 0.0)
        sh *= 2
    return x


def _without_refs(kernel, start, count):
    if count == 0:
        return kernel

    def wrapped(*refs):
        return kernel(*refs[:start], *refs[start + count:])
    return wrapped


def _rmsnorm_kernel(x_ref, g_ref, o_ref):
    x = x_ref[...]
    y = x * lax.rsqrt(jnp.mean(x * x, axis=-1, keepdims=True) + EPS)
    o_ref[...] = (y * g_ref[...]).astype(o_ref.dtype)


def _rmsnorm(x, g, out_dtype, row0=0, nrows=None):
    d = x.shape[1]
    nrows = x.shape[0] if nrows is None else nrows
    tm = _tile(math.gcd(row0, nrows), 256, 16)
    rb0 = row0 // tm
    return pl.pallas_call(
        _rmsnorm_kernel,
        grid=(nrows // tm,),
        in_specs=[pl.BlockSpec((tm, d), lambda i: (rb0 + i, 0)),
                  pl.BlockSpec((1, d), lambda i: (0, 0))],
        out_specs=pl.BlockSpec((tm, d), lambda i: (i, 0)),
        out_shape=jax.ShapeDtypeStruct((nrows, d), out_dtype),
        compiler_params=_params("parallel"),
        name="rmsnorm",
    )(x, g.reshape(1, d))


def _ffn_up_kernel(a_ref, wg_ref, wu_ref, o_ref):
    a = a_ref[...]
    gate = jnp.dot(a, wg_ref[...].astype(BF16), preferred_element_type=F32)
    up = jnp.dot(a, wu_ref[...].astype(BF16), preferred_element_type=F32)
    o_ref[...] = (gate * _sigmoid(gate) * up).astype(o_ref.dtype)


def _ffn_up(a, w_in, l):
    m, d = a.shape
    f = w_in.shape[2] // 2
    tm = _tile(m, 1536, 16)
    tn = _tile(f, 256, LANE)
    nf = f // tn
    return pl.pallas_call(
        _ffn_up_kernel,
        grid=(m // tm, nf),
        in_specs=[pl.BlockSpec((tm, d), lambda i, j: (i, 0)),
                  pl.BlockSpec((None, d, tn), lambda i, j: (l, 0, j)),
                  pl.BlockSpec((None, d, tn), lambda i, j: (l, 0, j + nf))],
        out_specs=pl.BlockSpec((tm, tn), lambda i, j: (i, j)),
        out_shape=jax.ShapeDtypeStruct((m, f), BF16),
        compiler_params=_params("parallel", "arbitrary"),
        name="ffn_up",
    )(a, w_in, w_in)


def _mm_res_kernel(a_ref, w_ref, r_ref, o_ref, *, scale):
    acc = jnp.dot(a_ref[...], w_ref[...], preferred_element_type=F32)
    o_ref[...] = r_ref[...] + scale * acc


def _mm_res(a, w, l, r, scale):
    m, k = a.shape
    _, nt, _, tn = w.shape
    tm = _tile(m, 768, 16)
    return pl.pallas_call(
        functools.partial(_mm_res_kernel, scale=scale),
        grid=(m // tm, nt),
        in_specs=[pl.BlockSpec((tm, k), lambda i, j: (i, 0)),
                  pl.BlockSpec((None, None, k, tn), lambda i, j: (l, j, 0, 0)),
                  pl.BlockSpec((tm, tn), lambda i, j: (i, j))],
        out_specs=pl.BlockSpec((tm, tn), lambda i, j: (i, j)),
        out_shape=jax.ShapeDtypeStruct((m, nt * tn), F32),
        compiler_params=_params("parallel", "arbitrary"),
        name="mm_res",
    )(a, w, r)


def _mm_kernel(a_ref, w_ref, o_ref, *, gate):
    acc = jnp.dot(a_ref[...], w_ref[...].astype(BF16), preferred_element_type=F32)
    o_ref[...] = (_sigmoid(acc) if gate else acc).astype(o_ref.dtype)


def _mm(a, w, l, col0, ncols, out_dtype, gate=False, tm_target=768, tn_target=1024):
    m, k = a.shape
    tm = _tile(m, tm_target, 16)
    tn = _tile(math.gcd(col0, ncols), tn_target, LANE)
    cb0 = col0 // tn
    return pl.pallas_call(
        functools.partial(_mm_kernel, gate=gate),
        grid=(m // tm, ncols // tn),
        in_specs=[pl.BlockSpec((tm, k), lambda i, j: (i, 0)),
                  pl.BlockSpec((None, k, tn), lambda i, j: (l, 0, cb0 + j))],
        out_specs=pl.BlockSpec((tm, tn), lambda i, j: (i, j)),
        out_shape=jax.ShapeDtypeStruct((m, ncols), out_dtype),
        compiler_params=_params("parallel", "arbitrary"),
        name="mm_gate" if gate else "mm",
    )(a, w)


def _kv_kernel(a_ref, wk_ref, wv_ref, kf_ref, vf_ref, kb_ref, vb_ref, *, transpose_v):
    a = a_ref[...]
    k = jnp.dot(a, wk_ref[...], preferred_element_type=F32)
    v = jnp.dot(a, wv_ref[...], preferred_element_type=F32)
    kf_ref[...] = k
    vf_ref[...] = v
    kb_ref[...] = k.astype(BF16)
    vb_ref[...] = (v.T if transpose_v else v).astype(BF16)


def _kv_proj(a, w_kv, l, row0, nrows, depth, prev, transpose_v):
    d = a.shape[1]
    _, nt2, _, tn = w_kv.shape
    nt = nt2 // 2
    d_c = nt * tn
    tm = _tile(math.gcd(row0, nrows), 512, 16)
    rb0 = row0 // tm
    stacked = jax.ShapeDtypeStruct((depth, nrows, d_c), F32)
    layer = jax.ShapeDtypeStruct((nrows, d_c), BF16)
    n_prev = 0 if prev is None else 2
    stack_spec = pl.BlockSpec((None, tm, tn), lambda i, j: (l, i, j))
    layer_spec = pl.BlockSpec((tm, tn), lambda i, j: (i, j))
    if transpose_v:
        v_shape = jax.ShapeDtypeStruct((d_c, nrows), BF16)
        v_spec = pl.BlockSpec((tn, tm), lambda i, j: (j, i))
    else:
        v_shape, v_spec = layer, layer_spec
    return pl.pallas_call(
        _without_refs(functools.partial(_kv_kernel, transpose_v=transpose_v), 3, n_prev),
        grid=(nrows // tm, nt),
        in_specs=[pl.BlockSpec((tm, d), lambda i, j: (rb0 + i, 0)),
                  pl.BlockSpec((None, None, d, tn), lambda i, j: (l, j, 0, 0)),
                  pl.BlockSpec((None, None, d, tn), lambda i, j: (l, nt + j, 0, 0))] + [ANY_SPEC] * n_prev,
        out_specs=[stack_spec, stack_spec, layer_spec, v_spec],
        out_shape=[stacked, stacked, layer, v_shape],
        input_output_aliases={3: 0, 4: 1} if prev is not None else {},
        compiler_params=_params("parallel", "arbitrary"),
        name="kv_proj",
    )(a, w_kv, w_kv, *(prev or ()))


def _branch_kernel(oa_ref, ob_ref, oc_ref, wa_ref, wb_ref, wc_ref,
                   ga_ref, gb_ref, gc_ref, o_ref):
    dot = functools.partial(jnp.dot, preferred_element_type=F32)
    merged = (ga_ref[...] * dot(oa_ref[...], wa_ref[...])
              + gb_ref[...] * dot(ob_ref[...], wb_ref[...])
              + gc_ref[...] * dot(oc_ref[...], wc_ref[...]))
    o_ref[...] = merged.astype(o_ref.dtype)


def _branch_merge(oa, ob, oc, wa, wb, wc, l, g):
    m = oa.shape[0]
    _, nd, _, tn = wa.shape
    tm = _tile(m, 768, 16)
    a_spec = lambda x: pl.BlockSpec((tm, x.shape[1]), lambda i, j: (i, 0))
    w_spec = lambda x: pl.BlockSpec((None, None, x.shape[2], tn), lambda i, j: (l, j, 0, 0))
    g_spec = lambda b: pl.BlockSpec((tm, tn), lambda i, j: (i, j + b * nd))
    return pl.pallas_call(
        _branch_kernel,
        grid=(m // tm, nd),
        in_specs=[a_spec(oa), a_spec(ob), a_spec(oc), w_spec(wa), w_spec(wb), w_spec(wc),
                  g_spec(0), g_spec(1), g_spec(2)],
        out_specs=pl.BlockSpec((tm, tn), lambda i, j: (i, j)),
        out_shape=jax.ShapeDtypeStruct((m, nd * tn), BF16),
        compiler_params=_params("parallel", "arbitrary"),
        name="branch_merge",
    )(oa, ob, oc, wa, wb, wc, g, g, g)


def _lower_bound_kernel(x_ref, o_ref):
    x = x_ref[...]
    depth = x.shape[0]
    e = jnp.exp(x - jnp.max(x, axis=0, keepdims=True))
    w = e / jnp.sum(e, axis=0, keepdims=True)
    run = w[0:1, :]
    rows = [run - w[0:1, :]]
    for l in range(1, depth):
        run = run + w[l:l + 1, :]
        rows.append(run - w[0:1, :])
    o_ref[...] = jnp.concatenate(rows, axis=0)


def _lower_bounds(logits):
    return pl.pallas_call(
        _lower_bound_kernel,
        out_shape=jax.ShapeDtypeStruct(logits.shape, F32),
        name="hgrn_lower_bounds",
    )(logits.astype(F32))


def _hgrn_kernel(q_ref, f_ref, i_ref, g_ref, lb_ref, ng_ref, s0_ref,
                 o_ref, sout_ref, st_scr, *, c, cpb, hpb):
    j = pl.program_id(2)

    @pl.when(j == 0)
    def _():
        for hh in range(hpb):
            st_scr[hh] = s0_ref[0, hh].T

    rowi = lax.broadcasted_iota(jnp.int32, (c, LANE), 0)
    ti = lax.broadcasted_iota(jnp.int32, (c, c), 0)
    si = lax.broadcasted_iota(jnp.int32, (c, c), 1)
    tx = ti ^ si
    eye = ti == si
    levels = []
    hb = 1
    while hb < c:
        levels.append(hb)
        hb *= 2
    masks = [(ti > si) & (tx >= hb) & (tx < 2 * hb) for hb in levels]
    dotf = functools.partial(lax.dot_general, preferred_element_type=F32)

    def level_ref(b, hb):
        n = 2 * hb
        if n == 2:
            return jnp.where((rowi & 1) != 0, pltpu.roll(b, 1, 0), b)
        if n == 4:
            m = rowi & 3
            return jnp.where(m == 0, pltpu.roll(b, c - 1, 0),
                             jnp.where(m == 1, b,
                                       jnp.where(m == 2, pltpu.roll(b, 1, 0),
                                                 pltpu.roll(b, 2, 0))))
        pieces = [jnp.broadcast_to(b[base + hb - 1:base + hb, :], (n, LANE))
                  for base in range(0, c, n)]
        return pieces[0] if len(pieces) == 1 else jnp.concatenate(pieces, axis=0)

    def head_chunk(rows, hh):
        cs = slice(hh * LANE, (hh + 1) * LANE)
        lb = lb_ref[:, cs]
        f = lb + (1.0 - lb) * _sigmoid(f_ref[rows, cs])
        kk = 1.0 - f
        qr = q_ref[rows, cs]
        q = qr * _sigmoid(qr)
        iv = i_ref[rows, cs].astype(BF16)
        b = _row_cumsum(jnp.log(f) * LOG2E)
        b_last = b[c - 1:c, :]
        st = st_scr[hh]
        o = dotf((q * jnp.exp2(b)).astype(BF16), st.astype(BF16), NT)
        a = jnp.where(eye, dotf(q.astype(BF16), kk.astype(BF16), NT), 0.0)
        for hb, mask in zip(levels, masks):
            upper = (rowi & hb) != 0
            d = b - level_ref(b, hb)
            x = (jnp.where(upper, q, kk) * jnp.exp2(jnp.where(upper, d, -d))).astype(BF16)
            a = a + jnp.where(mask, dotf(x, x, NT), 0.0)
        o = o + jnp.dot(a.astype(BF16), iv, preferred_element_type=F32)
        kb = kk * jnp.exp2(b_last - b)
        st_scr[hh] = st * jnp.exp2(b_last) + dotf(iv, kb.astype(BF16), TN)
        o = o * lax.rsqrt(jnp.mean(o * o, axis=-1, keepdims=True) + EPS) * ng_ref[:, cs]
        gr = g_ref[rows, cs]
        o_ref[rows, cs] = (o * (gr * _sigmoid(gr))).astype(o_ref.dtype)

    for ci in range(cpb):
        for hh in range(hpb):
            head_chunk(slice(ci * c, (ci + 1) * c), hh)

    @pl.when(j == pl.num_programs(2) - 1)
    def _():
        for hh in range(hpb):
            sout_ref[0, hh] = st_scr[hh].T


def _hgrn(p, row0, nstream, t, heads, lb, norm_g, s0, prev):
    m = p.shape[0]
    c = min(t, HGRN_CHUNK)
    rows = _tile(t, HGRN_ROWS_PER_STEP, c)
    cpb = rows // c
    nj = t // rows
    rb0 = row0 // rows
    hpb = _tile(heads, HGRN_HEADS_PER_STEP, 1)
    ng = heads // hpb
    w = hpb * LANE
    col = lambda part: (lambda b, hg, j: (rb0 + b * nj + j, part * ng + hg))
    vec = pl.BlockSpec((1, w), lambda b, hg, j: (0, hg))
    st_spec = pl.BlockSpec((1, hpb, LANE, LANE), lambda b, hg, j: (b, hg, 0, 0))
    n_prev = 0 if prev is None else 1
    o, s_out = pl.pallas_call(
        _without_refs(functools.partial(_hgrn_kernel, c=c, cpb=cpb, hpb=hpb), 7, n_prev),
        grid=(nstream, ng, nj),
        in_specs=[pl.BlockSpec((rows, w), col(0)), pl.BlockSpec((rows, w), col(1)),
                  pl.BlockSpec((rows, w), col(2)), pl.BlockSpec((rows, w), col(3)),
                  vec, vec, st_spec] + [ANY_SPEC] * n_prev,
        out_specs=[pl.BlockSpec((rows, w), lambda b, hg, j: (rb0 + b * nj + j, hg)), st_spec],
        out_shape=[jax.ShapeDtypeStruct((m, heads * LANE), BF16),
                   jax.ShapeDtypeStruct((nstream, heads, LANE, LANE), F32)],
        scratch_shapes=[pltpu.VMEM((hpb, LANE, LANE), F32)],
        input_output_aliases={7: 0} if prev is not None else {},
        compiler_params=_params("parallel", "parallel", "arbitrary"),
        name="hgrn2",
    )(p, p, p, p, lb.reshape(1, -1), norm_g.reshape(1, -1), s0, *(() if prev is None else (prev,)))
    return o, s_out


def _gmlp_kernel(u_ref, v_ref, ng_ref, w_ref, bst_ref, o_ref, *vout_ref, groups):
    v = v_ref[...]
    vn = v * lax.rsqrt(jnp.mean(v * v, axis=-1, keepdims=True) + EPS) * ng_ref[...]
    if vout_ref:
        vout_ref[0][...] = vn
    c = v.shape[0]
    tril = lax.broadcasted_iota(jnp.int32, (c, c), 0) >= lax.broadcasted_iota(jnp.int32, (c, c), 1)
    for g in range(groups):
        cs = slice(g * LANE, (g + 1) * LANE)
        w = jnp.where(tril, w_ref[g], 0.0).astype(BF16)
        s = jnp.dot(w, vn[:, cs].astype(BF16), preferred_element_type=F32) + bst_ref[:, g:g + 1]
        o_ref[:, cs] = (u_ref[:, cs] * s).astype(o_ref.dtype)


def _gmlp(p, row0, nrows, t, col0, d_b, norm_g, w_s, b_s, prev):
    m = p.shape[0]
    groups = w_s.shape[0]
    c = min(t, GMLP_CHUNK)
    w = w_s[:, :c, :c]
    bst = jnp.transpose(b_s[:, :c])
    rb0 = row0 // c
    cb = col0 // d_b
    emit_v = prev is not None
    out_shape = [jax.ShapeDtypeStruct((m, d_b), BF16)]
    out_specs = [pl.BlockSpec((c, d_b), lambda i: (rb0 + i, 0))]
    if emit_v:
        out_shape.append(jax.ShapeDtypeStruct((nrows, d_b), F32))
        out_specs.append(pl.BlockSpec((c, d_b), lambda i: (i, 0)))
    n_prev = 0 if prev is None else 1
    return pl.pallas_call(
        _without_refs(functools.partial(_gmlp_kernel, groups=groups), 5, n_prev),
        grid=(nrows // c,),
        in_specs=[pl.BlockSpec((c, d_b), lambda i: (rb0 + i, cb)),
                  pl.BlockSpec((c, d_b), lambda i: (rb0 + i, cb + 1)),
                  pl.BlockSpec((1, d_b), lambda i: (0, 0)),
                  pl.BlockSpec((groups, c, c), lambda i: (0, 0, 0)),
                  pl.BlockSpec((c, groups), lambda i: (0, 0))] + [ANY_SPEC] * n_prev,
        out_specs=out_specs,
        out_shape=out_shape,
        input_output_aliases={5: 0} if prev is not None else {},
        compiler_params=_params("parallel"),
        name="gmlp",
    )(p, p, norm_g.reshape(1, d_b), w, bst, *(() if prev is None else (prev,)))


def _log_sigmoid(z):
    return jnp.minimum(z, 0.0) - jnp.log1p(jnp.exp(-jnp.abs(z)))


def _fox_prep_kernel(cf_ref, bias_ref, logf_ref, f_ref, ft_ref, carry, *, heads):
    @pl.when(pl.program_id(1) == 0)
    def _():
        carry[...] = jnp.zeros_like(carry)

    logf = _log_sigmoid(cf_ref[...] + bias_ref[...])
    logf_ref[...] = logf[:, :heads]
    fc = carry[...] + _row_cumsum(logf)
    carry[...] = fc[fc.shape[0] - 1:, :]
    fc = fc * LOG2E
    f_ref[...] = fc
    ft_ref[0] = fc.T[:heads, :]


def _fox_prep(cf, bias, nstream, t, heads):
    tb = _tile(t, 256, LANE)
    nj = t // tb
    return pl.pallas_call(
        functools.partial(_fox_prep_kernel, heads=heads),
        grid=(nstream, nj),
        in_specs=[pl.BlockSpec((tb, LANE), lambda b, j: (b * nj + j, 0)),
                  pl.BlockSpec((1, LANE), lambda b, j: (0, 0))],
        out_specs=[pl.BlockSpec((tb, heads), lambda b, j: (b * nj + j, 0)),
                   pl.BlockSpec((tb, LANE), lambda b, j: (b * nj + j, 0)),
                   pl.BlockSpec((1, heads, tb), lambda b, j: (b, 0, j))],
        out_shape=[jax.ShapeDtypeStruct((nstream * t, heads), F32),
                   jax.ShapeDtypeStruct((nstream * t, LANE), F32),
                   jax.ShapeDtypeStruct((nstream, heads, t), F32)],
        scratch_shapes=[pltpu.VMEM((1, LANE), F32)],
        compiler_params=_params("parallel", "arbitrary"),
        name="fox_prep",
    )(cf, bias)


def _fox_attn_kernel(q_ref, k_ref, vt_ref, fq_ref, fk_ref, o_ref,
                     m_scr, l_scr, acc_scr, *, heads, scale, blk, ks):
    i = pl.program_id(1)
    j = pl.program_id(2)
    nsub = blk // ks

    @pl.when(j == 0)
    def _():
        m_scr[...] = jnp.full_like(m_scr, -jnp.inf)
        l_scr[...] = jnp.zeros_like(l_scr)
        acc_scr[...] = jnp.zeros_like(acc_scr)

    def step(diagonal):
        if diagonal:
            key = lax.broadcasted_iota(jnp.int32, (ks, blk), 0)
            qry = lax.broadcasted_iota(jnp.int32, (ks, blk), 1)
            hide = [jnp.where(qry >= key + u * ks, 0.0, -jnp.inf) for u in range(nsub)]
        for h in range(heads):
            cs = slice(h * LANE, (h + 1) * LANE)
            fq = fq_ref[0, h:h + 1, :]
            q = q_ref[:, cs]
            for u in range(nsub):
                rs = slice(u * ks, (u + 1) * ks)
                z = lax.dot_general(k_ref[rs, cs], q, NT,
                                    preferred_element_type=F32) * scale - fk_ref[rs, h:h + 1]
                if diagonal:
                    z = z + hide[u]
                m_prev = m_scr[h]
                m_new = jnp.maximum(m_prev, jnp.max(z, axis=0, keepdims=True) + fq)
                alpha = jnp.exp2(m_prev - m_new)
                p = jnp.exp2(z + (fq - m_new))
                l_scr[h] = alpha * l_scr[h] + jnp.sum(p, axis=0, keepdims=True)
                acc_scr[h] = alpha * acc_scr[h] + jnp.dot(
                    vt_ref[cs, rs], p.astype(BF16), preferred_element_type=F32)
                m_scr[h] = m_new

    @pl.when(j < i)
    def _():
        step(False)

    @pl.when(j == i)
    def _():
        step(True)
        for h in range(heads):
            cs = slice(h * LANE, (h + 1) * LANE)
            o_ref[:, cs] = (acc_scr[h] / l_scr[h]).T.astype(o_ref.dtype)


def _fox_attn_prompt(q, k, vt, m_total, nstream, t, heads, f_col, f_row):
    d_c = heads * LANE
    blk = _tile(t, ATTN_BLOCK, LANE)
    ks = _tile(blk, ATTN_KEY_SUB, LANE)
    nb = t // blk
    kv_row = lambda b, i, j: b * nb + jnp.minimum(j, i)
    stat = pltpu.VMEM((heads, 1, blk), F32)
    return pl.pallas_call(
        functools.partial(_fox_attn_kernel, heads=heads, scale=LANE ** -0.5 * LOG2E, blk=blk, ks=ks),
        grid=(nstream, nb, nb),
        in_specs=[pl.BlockSpec((blk, d_c), lambda b, i, j: (b * nb + i, 0)),
                  pl.BlockSpec((blk, d_c), lambda b, i, j: (kv_row(b, i, j), 0)),
                  pl.BlockSpec((d_c, blk), lambda b, i, j: (0, kv_row(b, i, j))),
                  pl.BlockSpec((1, heads, blk), lambda b, i, j: (b, 0, i)),
                  pl.BlockSpec((blk, LANE), lambda b, i, j: (kv_row(b, i, j), 0))],
        out_specs=pl.BlockSpec((blk, d_c), lambda b, i, j: (b * nb + i, 0)),
        out_shape=jax.ShapeDtypeStruct((m_total, d_c), BF16),
        scratch_shapes=[stat, stat, pltpu.VMEM((heads, LANE, blk), F32)],
        compiler_params=_params("parallel", "parallel", "arbitrary"),
        name="fox_attn_prompt",
    )(q, k, vt, f_row, f_col)


def _fox_sample_kernel(q_ref, k_ref, v_ref, cf_ref, bias_ref, pk_ref, pv_ref, plf_ref,
                       o_ref, lf_ref, *, heads, scale):
    ts = q_ref.shape[0]
    past = pk_ref.shape[1]
    f_past = _row_cumsum(plf_ref[0])
    logf = _log_sigmoid(cf_ref[...] + bias_ref[...])
    lf_ref[0] = logf[:, :heads]
    f_new = f_past[past - 1:past, :] + _row_cumsum(logf)
    f_past_t = f_past.T
    f_new_t = jnp.concatenate([f_new, jnp.zeros((LANE - ts, LANE), F32)], axis=0).T
    tril = lax.broadcasted_iota(jnp.int32, (ts, ts), 0) >= lax.broadcasted_iota(jnp.int32, (ts, ts), 1)
    dotf = functools.partial(lax.dot_general, preferred_element_type=F32)
    for h in range(heads):
        cs = slice(h * LANE, (h + 1) * LANE)
        qh = q_ref[:, cs]
        fq = f_new[:, h:h + 1]
        s_p = dotf(qh, pk_ref[0, :, cs].astype(BF16), NT) * scale + fq - f_past_t[h:h + 1, :]
        s_n = dotf(qh, k_ref[:, cs], NT) * scale + fq - f_new_t[h:h + 1, :ts]
        s_n = jnp.where(tril, s_n, -jnp.inf)
        m = jnp.maximum(jnp.max(s_p, axis=-1, keepdims=True), jnp.max(s_n, axis=-1, keepdims=True))
        e_p = jnp.exp(s_p - m)
        e_n = jnp.exp(s_n - m)
        l = jnp.sum(e_p, axis=-1, keepdims=True) + jnp.sum(e_n, axis=-1, keepdims=True)
        acc = (jnp.dot(e_p.astype(BF16), pv_ref[0, :, cs].astype(BF16), preferred_element_type=F32)
               + jnp.dot(e_n.astype(BF16), v_ref[:, cs], preferred_element_type=F32))
        o_ref[:, cs] = (acc / l).astype(o_ref.dtype)


def _fox_attn_sample(q, row0, k_new, v_new, cf, bias, past_k, past_v, past_logf, l,
                     nstream, ts, heads, prev):
    d_c = heads * LANE
    past = past_k.shape[2]
    rb0 = row0 // ts
    new = pl.BlockSpec((ts, d_c), lambda b: (b, 0))
    cache = pl.BlockSpec((None, 1, past, d_c), lambda b: (l, b, 0, 0))
    return pl.pallas_call(
        _without_refs(functools.partial(_fox_sample_kernel, heads=heads, scale=LANE ** -0.5), 8, 1),
        grid=(nstream,),
        in_specs=[pl.BlockSpec((ts, d_c), lambda b: (rb0 + b, 0)), new, new,
                  pl.BlockSpec((ts, LANE), lambda b: (rb0 + b, 0)),
                  pl.BlockSpec((1, LANE), lambda b: (0, 0)),
                  cache, cache,
                  pl.BlockSpec((None, 1, past, LANE), lambda b: (l, b, 0, 0)),
                  ANY_SPEC],
        out_specs=[pl.BlockSpec((ts, d_c), lambda b: (rb0 + b, 0)),
                   pl.BlockSpec((1, ts, heads), lambda b: (b, 0, 0))],
        out_shape=[jax.ShapeDtypeStruct(prev.shape, BF16),
                   jax.ShapeDtypeStruct((nstream, ts, heads), F32)],
        input_output_aliases={8: 0},
        compiler_params=_params("parallel"),
        name="fox_attn_sample",
    )(q, k_new, v_new, cf, bias, past_k, past_v, past_logf, prev)


def kernel(x_prompt, x_sample, state_hgrn, cache_k, cache_v, cache_logf, ffn1_norm, ffn1_w_in,
           ffn1_w_out, mix_norm, w_in, w_gate, hgrn_lb_logits, hgrn_norm, gmlp_norm, gmlp_w_s,
           gmlp_b_s, fox_bias, w_branch_a, w_branch_b, w_branch_c, w_out, ffn2_norm, ffn2_w_in,
           ffn2_w_out, final_norm):
    nb, seq, d = x_prompt.shape
    ns, ts, _ = x_sample.shape
    depth = state_hgrn.shape[0]
    a_heads = state_hgrn.shape[2]
    c_heads = cache_k.shape[3]
    past = cache_k.shape[2]
    d_a = a_heads * LANE
    d_b = gmlp_norm.shape[1]
    d_c = c_heads * LANE
    col_b = 4 * d_a
    col_q = col_b + 2 * d_b
    col_k = col_q + d_c
    n_main = col_k + 2 * d_c
    mp = nb * seq
    ms = ns * ts
    assert w_in.shape[2] == n_main + c_heads and c_heads <= LANE
    assert col_b % d_b == 0 and mp % ts == 0

    bf = lambda w: w.astype(BF16)
    w1_in, w2_in = ffn1_w_in, ffn2_w_in
    tn_down = _tile(d, 256, LANE)
    w1_out, w2_out = _tile_major(bf(ffn1_w_out), tn_down), _tile_major(bf(ffn2_w_out), tn_down)
    w_out_b = _tile_major(bf(w_out), _tile(d, 1024, LANE))
    w_in_b = bf(w_in[:, :, :col_k])
    w_kv = _tile_major(bf(w_in[:, :, col_k:n_main]), _tile(d_c, 512, LANE))
    tn_br = _tile(d, 512, LANE)
    wa, wb, wc = (_tile_major(bf(w), tn_br) for w in (w_branch_a, w_branch_b, w_branch_c))
    w_cf = bf(jnp.pad(w_in[:, :, n_main:], ((0, 0), (0, 0), (0, LANE - c_heads))))
    bias_all = jnp.pad(fox_bias.astype(F32), ((0, 0), (0, LANE - c_heads)))
    past_k = cache_k.reshape(depth, ns, past, d_c)
    past_v = cache_v.reshape(depth, ns, past, d_c)
    past_lf = jnp.pad(cache_logf.astype(F32), ((0, 0), (0, 0), (0, 0), (0, LANE - c_heads)))

    lower_bounds = _lower_bounds(hgrn_lb_logits)
    x = jnp.concatenate([x_prompt.reshape(mp, d), x_sample.reshape(ms, d)], axis=0)
    s0_prompt = jnp.zeros((nb, a_heads, LANE, LANE), F32)

    kv_p = kv_s = None
    outs = {name: [] for name in ("sa_p", "sa_s", "vb_s", "lf_p", "lf_s")}
    for l in range(depth):
        bias = bias_all[l:l + 1]

        hid = _ffn_up(_rmsnorm(x, ffn1_norm[l], BF16), w1_in, l)
        h = _mm_res(hid, w1_out, l, x, 0.5)

        n = _rmsnorm(h, mix_norm[l], BF16)
        p_ab = _mm(n, w_in_b, l, 0, col_q, F32)
        q_c = _mm(n, w_in_b, l, col_q, d_c, BF16)
        g = _mm(n, w_gate, l, 0, 3 * d, F32, gate=True, tm_target=1536, tn_target=512)
        cf = _mm(n, w_cf, l, 0, LANE, F32)
        kp_all, vp_all, kp, vpt = _kv_proj(n, w_kv, l, 0, mp, depth, kv_p, True)
        ks_all, vs_all, ks, vs = _kv_proj(n, w_kv, l, mp, ms, depth, kv_s, False)
        kv_p, kv_s = (kp_all, vp_all), (ks_all, vs_all)

        oa, sa_p = _hgrn(p_ab, 0, nb, seq, a_heads, lower_bounds[l], hgrn_norm[l], s0_prompt, None)
        oa, sa_s = _hgrn(p_ab, mp, ns, ts, a_heads, lower_bounds[l], hgrn_norm[l], state_hgrn[l], oa)
        (ob,) = _gmlp(p_ab, 0, mp, seq, col_b, d_b, gmlp_norm[l], gmlp_w_s[l], gmlp_b_s[l], None)
        ob, vb_s = _gmlp(p_ab, mp, ms, ts, col_b, d_b, gmlp_norm[l], gmlp_w_s[l], gmlp_b_s[l], ob)
        lf_p, f_col, f_row = _fox_prep(cf, bias, nb, seq, c_heads)
        oc = _fox_attn_prompt(q_c, kp, vpt, mp + ms, nb, seq, c_heads, f_col, f_row)
        oc, lf_s = _fox_attn_sample(q_c, mp, ks, vs, cf, bias, past_k, past_v, past_lf, l,
                                    ns, ts, c_heads, oc)

        merged = _branch_merge(oa, ob, oc, wa, wb, wc, l, g)
        h = _mm_res(merged, w_out_b, l, h, 1.0)

        hid = _ffn_up(_rmsnorm(h, ffn2_norm[l], BF16), w2_in, l)
        x = _mm_res(hid, w2_out, l, h, 0.5)

        outs["sa_p"].append(sa_p)
        outs["sa_s"].append(sa_s)
        outs["vb_s"].append(vb_s.reshape(ns, ts, d_b))
        outs["lf_p"].append(lf_p.reshape(nb, seq, c_heads))
        outs["lf_s"].append(lf_s)

    y_p = _rmsnorm(x, final_norm, F32, 0, mp)
    y_s = _rmsnorm(x, final_norm, F32, mp, ms)
    stack = lambda name: jnp.stack(outs[name])
    return (y_p.reshape(nb, seq, d), y_s.reshape(ns, ts, d),
            stack("sa_p"), stack("sa_s"), stack("vb_s"),
            kv_p[0].reshape(depth, nb, seq, c_heads, LANE), kv_s[0].reshape(depth, ns, ts, c_heads, LANE),
            kv_p[1].reshape(depth, nb, seq, c_heads, LANE), kv_s[1].reshape(depth, ns, ts, c_heads, LANE),
            stack("lf_p"), stack("lf_s"))
```

```python
import functools
import math

import jax
import jax.numpy as jnp
from jax import lax
from jax.experimental import pallas as pl
from jax.experimental.pallas import tpu as pltpu

EPS = 1e-6
LOG2E = 1.4426950408889634
LANE = 128
HGRN_CHUNK = 128
HGRN_ROWS_PER_STEP = 256
HGRN_HEADS_PER_STEP = 8
GMLP_CHUNK = 128
ATTN_BLOCK = 256
ATTN_KEY_SUB = 128
VMEM_LIMIT_BYTES = 56 * 1024 * 1024

F32 = jnp.float32
BF16 = jnp.bfloat16
NT = (((1,), (1,)), ((), ()))
TN = (((0,), (0,)), ((), ()))
ANY_SPEC = pl.BlockSpec(memory_space=pl.ANY)


def _tile(n, target, align):
    t = (min(target, n) // align) * align
    while t >= align:
        if n % t == 0:
            return t
        t -= align
    return n


def _params(*sem):
    return pltpu.CompilerParams(dimension_semantics=sem, vmem_limit_bytes=VMEM_LIMIT_BYTES)


def _sigmoid(x):
    return jax.nn.sigmoid(x)


def _row_cumsum(x):
    n = x.shape[0]
    row = lax.broadcasted_iota(jnp.int32, x.shape, 0)
    sh = 1
    while sh < n:
        x = x + jnp.where(row >= sh, pltpu.roll(x, sh, 0), 0.0)
        sh *= 2
    return x


def _without_refs(kernel, start, count):
    if count == 0:
        return kernel

    def wrapped(*refs):
        return kernel(*refs[:start], *refs[start + count:])
    return wrapped


def _rmsnorm_kernel(x_ref, g_ref, o_ref):
    x = x_ref[...]
    y = x * lax.rsqrt(jnp.mean(x * x, axis=-1, keepdims=True) + EPS)
    o_ref[...] = (y * g_ref[...]).astype(o_ref.dtype)


def _rmsnorm(x, g, out_dtype, row0=0, nrows=None):
    d = x.shape[1]
    nrows = x.shape[0] if nrows is None else nrows
    tm = _tile(math.gcd(row0, nrows), 256, 16)
    rb0 = row0 // tm
    return pl.pallas_call(
        _rmsnorm_kernel,
        grid=(nrows // tm,),
        in_specs=[pl.BlockSpec((tm, d), lambda i: (rb0 + i, 0)),
                  pl.BlockSpec((1, d), lambda i: (0, 0))],
        out_specs=pl.BlockSpec((tm, d), lambda i: (i, 0)),
        out_shape=jax.ShapeDtypeStruct((nrows, d), out_dtype),
        compiler_params=_params("parallel"),
        name="rmsnorm",
    )(x, g.reshape(1, d))


def _ffn_up_kernel(a_ref, wg_ref, wu_ref, o_ref):
    a = a_ref[...]
    gate = jnp.dot(a, wg_ref[...].astype(BF16), preferred_element_type=F32)
    up = jnp.dot(a, wu_ref[...].astype(BF16), preferred_element_type=F32)
    o_ref[...] = (gate * _sigmoid(gate) * up).astype(o_ref.dtype)


def _ffn_up(a, w_in, l):
    m, d = a.shape
    f = w_in.shape[2] // 2
    tm = _tile(m, 1536, 16)
    tn = _tile(f, 256, LANE)
    nf = f // tn
    return pl.pallas_call(
        _ffn_up_kernel,
        grid=(m // tm, nf),
        in_specs=[pl.BlockSpec((tm, d), lambda i, j: (i, 0)),
                  pl.BlockSpec((None, d, tn), lambda i, j: (l, 0, j)),
                  pl.BlockSpec((None, d, tn), lambda i, j: (l, 0, j + nf))],
        out_specs=pl.BlockSpec((tm, tn), lambda i, j: (i, j)),
        out_shape=jax.ShapeDtypeStruct((m, f), BF16),
        compiler_params=_params("parallel", "arbitrary"),
        name="ffn_up",
    )(a, w_in, w_in)


def _mm_res_kernel(a_ref, w_ref, r_ref, o_ref, *, scale):
    acc = jnp.dot(a_ref[...], w_ref[...], preferred_element_type=F32)
    o_ref[...] = r_ref[...] + scale * acc


def _mm_res(a, w, l, r, scale, tn_target):
    m, k = a.shape
    n = w.shape[2]
    tm = _tile(m, 768, 16)
    tn = _tile(n, tn_target, LANE)
    return pl.pallas_call(
        functools.partial(_mm_res_kernel, scale=scale),
        grid=(m // tm, n // tn),
        in_specs=[pl.BlockSpec((tm, k), lambda i, j: (i, 0)),
                  pl.BlockSpec((None, k, tn), lambda i, j: (l, 0, j)),
                  pl.BlockSpec((tm, tn), lambda i, j: (i, j))],
        out_specs=pl.BlockSpec((tm, tn), lambda i, j: (i, j)),
        out_shape=jax.ShapeDtypeStruct((m, n), F32),
        compiler_params=_params("parallel", "arbitrary"),
        name="mm_res",
    )(a, w, r)


def _mm_kernel(a_ref, w_ref, o_ref, *, gate):
    acc = jnp.dot(a_ref[...], w_ref[...], preferred_element_type=F32)
    o_ref[...] = (_sigmoid(acc) if gate else acc).astype(o_ref.dtype)


def _mm(a, w, l, col0, ncols, out_dtype, gate=False, tn_target=1024):
    m, k = a.shape
    tm = _tile(m, 768, 16)
    tn = _tile(math.gcd(col0, ncols), tn_target, LANE)
    cb0 = col0 // tn
    return pl.pallas_call(
        functools.partial(_mm_kernel, gate=gate),
        grid=(m // tm, ncols // tn),
        in_specs=[pl.BlockSpec((tm, k), lambda i, j: (i, 0)),
                  pl.BlockSpec((None, k, tn), lambda i, j: (l, 0, cb0 + j))],
        out_specs=pl.BlockSpec((tm, tn), lambda i, j: (i, j)),
        out_shape=jax.ShapeDtypeStruct((m, ncols), out_dtype),
        compiler_params=_params("parallel", "arbitrary"),
        name="mm_gate" if gate else "mm",
    )(a, w)


def _kv_kernel(a_ref, wk_ref, wv_ref, kf_ref, vf_ref, kb_ref, vb_ref, *, transpose_v):
    a = a_ref[...]
    k = jnp.dot(a, wk_ref[...], preferred_element_type=F32)
    v = jnp.dot(a, wv_ref[...], preferred_element_type=F32)
    kf_ref[...] = k
    vf_ref[...] = v
    kb_ref[...] = k.astype(BF16)
    vb_ref[...] = (v.T if transpose_v else v).astype(BF16)


def _kv_proj(a, w, l, col_k, d_c, row0, nrows, depth, prev, transpose_v):
    d = a.shape[1]
    tm = _tile(math.gcd(row0, nrows), 512, 16)
    tn = _tile(d_c, 512, LANE)
    rb0 = row0 // tm
    ck = col_k // tn
    cv = (col_k + d_c) // tn
    stacked = jax.ShapeDtypeStruct((depth, nrows, d_c), F32)
    layer = jax.ShapeDtypeStruct((nrows, d_c), BF16)
    n_prev = 0 if prev is None else 2
    stack_spec = pl.BlockSpec((None, tm, tn), lambda i, j: (l, i, j))
    layer_spec = pl.BlockSpec((tm, tn), lambda i, j: (i, j))
    if transpose_v:
        v_shape = jax.ShapeDtypeStruct((d_c, nrows), BF16)
        v_spec = pl.BlockSpec((tn, tm), lambda i, j: (j, i))
    else:
        v_shape, v_spec = layer, layer_spec
    return pl.pallas_call(
        _without_refs(functools.partial(_kv_kernel, transpose_v=transpose_v), 3, n_prev),
        grid=(nrows // tm, d_c // tn),
        in_specs=[pl.BlockSpec((tm, d), lambda i, j: (rb0 + i, 0)),
                  pl.BlockSpec((None, d, tn), lambda i, j: (l, 0, ck + j)),
                  pl.BlockSpec((None, d, tn), lambda i, j: (l, 0, cv + j))] + [ANY_SPEC] * n_prev,
        out_specs=[stack_spec, stack_spec, layer_spec, v_spec],
        out_shape=[stacked, stacked, layer, v_shape],
        input_output_aliases={3: 0, 4: 1} if prev is not None else {},
        compiler_params=_params("parallel", "arbitrary"),
        name="kv_proj",
    )(a, w, w, *(prev or ()))


def _branch_kernel(oa_ref, ob_ref, oc_ref, wa_ref, wb_ref, wc_ref,
                   ga_ref, gb_ref, gc_ref, o_ref):
    dot = functools.partial(jnp.dot, preferred_element_type=F32)
    merged = (ga_ref[...] * dot(oa_ref[...], wa_ref[...])
              + gb_ref[...] * dot(ob_ref[...], wb_ref[...])
              + gc_ref[...] * dot(oc_ref[...], wc_ref[...]))
    o_ref[...] = merged.astype(o_ref.dtype)


def _branch_merge(oa, ob, oc, wa, wb, wc, l, g):
    m = oa.shape[0]
    d = wa.shape[2]
    tm = _tile(m, 768, 16)
    tn = _tile(d, 512, LANE)
    nd = d // tn
    a_spec = lambda x: pl.BlockSpec((tm, x.shape[1]), lambda i, j: (i, 0))
    w_spec = lambda x: pl.BlockSpec((None, x.shape[1], tn), lambda i, j: (l, 0, j))
    g_spec = lambda b: pl.BlockSpec((tm, tn), lambda i, j: (i, j + b * nd))
    return pl.pallas_call(
        _branch_kernel,
        grid=(m // tm, nd),
        in_specs=[a_spec(oa), a_spec(ob), a_spec(oc), w_spec(wa), w_spec(wb), w_spec(wc),
                  g_spec(0), g_spec(1), g_spec(2)],
        out_specs=pl.BlockSpec((tm, tn), lambda i, j: (i, j)),
        out_shape=jax.ShapeDtypeStruct((m, d), BF16),
        compiler_params=_params("parallel", "arbitrary"),
        name="branch_merge",
    )(oa, ob, oc, wa, wb, wc, g, g, g)


def _lower_bound_kernel(x_ref, o_ref):
    x = x_ref[...]
    depth = x.shape[0]
    e = jnp.exp(x - jnp.max(x, axis=0, keepdims=True))
    w = e / jnp.sum(e, axis=0, keepdims=True)
    run = w[0:1, :]
    rows = [run - w[0:1, :]]
    for l in range(1, depth):
        run = run + w[l:l + 1, :]
        rows.append(run - w[0:1, :])
    o_ref[...] = jnp.concatenate(rows, axis=0)


def _lower_bounds(logits):
    return pl.pallas_call(
        _lower_bound_kernel,
        out_shape=jax.ShapeDtypeStruct(logits.shape, F32),
        name="hgrn_lower_bounds",
    )(logits.astype(F32))


def _hgrn_kernel(q_ref, f_ref, i_ref, g_ref, lb_ref, ng_ref, s0_ref,
                 o_ref, sout_ref, st_scr, *, c, cpb, hpb):
    j = pl.program_id(2)

    @pl.when(j == 0)
    def _():
        for hh in range(hpb):
            st_scr[hh] = s0_ref[0, hh].T

    rowi = lax.broadcasted_iota(jnp.int32, (c, LANE), 0)
    ti = lax.broadcasted_iota(jnp.int32, (c, c), 0)
    si = lax.broadcasted_iota(jnp.int32, (c, c), 1)
    tx = ti ^ si
    eye = ti == si
    levels = []
    hb = 1
    while hb < c:
        levels.append(hb)
        hb *= 2
    masks = [(ti > si) & (tx >= hb) & (tx < 2 * hb) for hb in levels]
    dotf = functools.partial(lax.dot_general, preferred_element_type=F32)

    def level_ref(b, hb):
        n = 2 * hb
        if n == 2:
            return jnp.where((rowi & 1) != 0, pltpu.roll(b, 1, 0), b)
        if n == 4:
            m = rowi & 3
            return jnp.where(m == 0, pltpu.roll(b, c - 1, 0),
                             jnp.where(m == 1, b,
                                       jnp.where(m == 2, pltpu.roll(b, 1, 0),
                                                 pltpu.roll(b, 2, 0))))
        pieces = [jnp.broadcast_to(b[base + hb - 1:base + hb, :], (n, LANE))
                  for base in range(0, c, n)]
        return pieces[0] if len(pieces) == 1 else jnp.concatenate(pieces, axis=0)

    def head_chunk(rows, hh):
        cs = slice(hh * LANE, (hh + 1) * LANE)
        lb = lb_ref[:, cs]
        f = lb + (1.0 - lb) * _sigmoid(f_ref[rows, cs])
        kk = 1.0 - f
        qr = q_ref[rows, cs]
        q = qr * _sigmoid(qr)
        iv = i_ref[rows, cs].astype(BF16)
        b = _row_cumsum(jnp.log(f) * LOG2E)
        b_last = b[c - 1:c, :]
        st = st_scr[hh]
        o = dotf((q * jnp.exp2(b)).astype(BF16), st.astype(BF16), NT)
        a = jnp.where(eye, dotf(q.astype(BF16), kk.astype(BF16), NT), 0.0)
        for hb, mask in zip(levels, masks):
            upper = (rowi & hb) != 0
            d = b - level_ref(b, hb)
            x = (jnp.where(upper, q, kk) * jnp.exp2(jnp.where(upper, d, -d))).astype(BF16)
            a = a + jnp.where(mask, dotf(x, x, NT), 0.0)
        o = o + jnp.dot(a.astype(BF16), iv, preferred_element_type=F32)
        kb = kk * jnp.exp2(b_last - b)
        st_scr[hh] = st * jnp.exp2(b_last) + dotf(iv, kb.astype(BF16), TN)
        o = o * lax.rsqrt(jnp.mean(o * o, axis=-1, keepdims=True) + EPS) * ng_ref[:, cs]
        gr = g_ref[rows, cs]
        o_ref[rows, cs] = (o * (gr * _sigmoid(gr))).astype(o_ref.dtype)

    for ci in range(cpb):
        for hh in range(hpb):
            head_chunk(slice(ci * c, (ci + 1) * c), hh)

    @pl.when(j == pl.num_programs(2) - 1)
    def _():
        for hh in range(hpb):
            sout_ref[0, hh] = st_scr[hh].T


def _hgrn(p, row0, nstream, t, heads, lb, norm_g, s0, prev):
    m = p.shape[0]
    c = min(t, HGRN_CHUNK)
    rows = _tile(t, HGRN_ROWS_PER_STEP, c)
    cpb = rows // c
    nj = t // rows
    rb0 = row0 // rows
    hpb = _tile(heads, HGRN_HEADS_PER_STEP, 1)
    ng = heads // hpb
    w = hpb * LANE
    col = lambda part: (lambda b, hg, j: (rb0 + b * nj + j, part * ng + hg))
    vec = pl.BlockSpec((1, w), lambda b, hg, j: (0, hg))
    st_spec = pl.BlockSpec((1, hpb, LANE, LANE), lambda b, hg, j: (b, hg, 0, 0))
    n_prev = 0 if prev is None else 1
    o, s_out = pl.pallas_call(
        _without_refs(functools.partial(_hgrn_kernel, c=c, cpb=cpb, hpb=hpb), 7, n_prev),
        grid=(nstream, ng, nj),
        in_specs=[pl.BlockSpec((rows, w), col(0)), pl.BlockSpec((rows, w), col(1)),
                  pl.BlockSpec((rows, w), col(2)), pl.BlockSpec((rows, w), col(3)),
                  vec, vec, st_spec] + [ANY_SPEC] * n_prev,
        out_specs=[pl.BlockSpec((rows, w), lambda b, hg, j: (rb0 + b * nj + j, hg)), st_spec],
        out_shape=[jax.ShapeDtypeStruct((m, heads * LANE), BF16),
                   jax.ShapeDtypeStruct((nstream, heads, LANE, LANE), F32)],
        scratch_shapes=[pltpu.VMEM((hpb, LANE, LANE), F32)],
        input_output_aliases={7: 0} if prev is not None else {},
        compiler_params=_params("parallel", "parallel", "arbitrary"),
        name="hgrn2",
    )(p, p, p, p, lb.reshape(1, -1), norm_g.reshape(1, -1), s0, *(() if prev is None else (prev,)))
    return o, s_out


def _gmlp_kernel(u_ref, v_ref, ng_ref, w_ref, bst_ref, o_ref, *vout_ref, groups):
    v = v_ref[...]
    vn = v * lax.rsqrt(jnp.mean(v * v, axis=-1, keepdims=True) + EPS) * ng_ref[...]
    if vout_ref:
        vout_ref[0][...] = vn
    c = v.shape[0]
    tril = lax.broadcasted_iota(jnp.int32, (c, c), 0) >= lax.broadcasted_iota(jnp.int32, (c, c), 1)
    for g in range(groups):
        cs = slice(g * LANE, (g + 1) * LANE)
        w = jnp.where(tril, w_ref[g], 0.0).astype(BF16)
        s = jnp.dot(w, vn[:, cs].astype(BF16), preferred_element_type=F32) + bst_ref[:, g:g + 1]
        o_ref[:, cs] = (u_ref[:, cs] * s).astype(o_ref.dtype)


def _gmlp(p, row0, nrows, t, col0, d_b, norm_g, w_s, b_s, prev):
    m = p.shape[0]
    groups = w_s.shape[0]
    c = min(t, GMLP_CHUNK)
    w = w_s[:, :c, :c]
    bst = jnp.transpose(b_s[:, :c])
    rb0 = row0 // c
    cb = col0 // d_b
    emit_v = prev is not None
    out_shape = [jax.ShapeDtypeStruct((m, d_b), BF16)]
    out_specs = [pl.BlockSpec((c, d_b), lambda i: (rb0 + i, 0))]
    if emit_v:
        out_shape.append(jax.ShapeDtypeStruct((nrows, d_b), F32))
        out_specs.append(pl.BlockSpec((c, d_b), lambda i: (i, 0)))
    n_prev = 0 if prev is None else 1
    return pl.pallas_call(
        _without_refs(functools.partial(_gmlp_kernel, groups=groups), 5, n_prev),
        grid=(nrows // c,),
        in_specs=[pl.BlockSpec((c, d_b), lambda i: (rb0 + i, cb)),
                  pl.BlockSpec((c, d_b), lambda i: (rb0 + i, cb + 1)),
                  pl.BlockSpec((1, d_b), lambda i: (0, 0)),
                  pl.BlockSpec((groups, c, c), lambda i: (0, 0, 0)),
                  pl.BlockSpec((c, groups), lambda i: (0, 0))] + [ANY_SPEC] * n_prev,
        out_specs=out_specs,
        out_shape=out_shape,
        input_output_aliases={5: 0} if prev is not None else {},
        compiler_params=_params("parallel"),
        name="gmlp",
    )(p, p, norm_g.reshape(1, d_b), w, bst, *(() if prev is None else (prev,)))


def _log_sigmoid(z):
    return jnp.minimum(z, 0.0) - jnp.log1p(jnp.exp(-jnp.abs(z)))


def _fox_prep_kernel(cf_ref, bias_ref, logf_ref, f_ref, ft_ref, carry, *, heads):
    @pl.when(pl.program_id(1) == 0)
    def _():
        carry[...] = jnp.zeros_like(carry)

    logf = _log_sigmoid(cf_ref[...] + bias_ref[...])
    logf_ref[...] = logf[:, :heads]
    fc = carry[...] + _row_cumsum(logf)
    carry[...] = fc[fc.shape[0] - 1:, :]
    fc = fc * LOG2E
    f_ref[...] = fc
    ft_ref[0] = fc.T[:heads, :]


def _fox_prep(cf, bias, nstream, t, heads):
    tb = _tile(t, 256, LANE)
    nj = t // tb
    return pl.pallas_call(
        functools.partial(_fox_prep_kernel, heads=heads),
        grid=(nstream, nj),
        in_specs=[pl.BlockSpec((tb, LANE), lambda b, j: (b * nj + j, 0)),
                  pl.BlockSpec((1, LANE), lambda b, j: (0, 0))],
        out_specs=[pl.BlockSpec((tb, heads), lambda b, j: (b * nj + j, 0)),
                   pl.BlockSpec((tb, LANE), lambda b, j: (b * nj + j, 0)),
                   pl.BlockSpec((1, heads, tb), lambda b, j: (b, 0, j))],
        out_shape=[jax.ShapeDtypeStruct((nstream * t, heads), F32),
                   jax.ShapeDtypeStruct((nstream * t, LANE), F32),
                   jax.ShapeDtypeStruct((nstream, heads, t), F32)],
        scratch_shapes=[pltpu.VMEM((1, LANE), F32)],
        compiler_params=_params("parallel", "arbitrary"),
        name="fox_prep",
    )(cf, bias)


def _fox_attn_kernel(qi_ref, kj_ref, q_ref, k_ref, vt_ref, fq_ref, fk_ref, o_ref,
                     m_scr, l_scr, acc_scr, *, heads, scale, blk, ks):
    t = pl.program_id(1)
    i = qi_ref[t]
    j = kj_ref[t]
    nsub = blk // ks

    @pl.when(j == 0)
    def _():
        m_scr[...] = jnp.full_like(m_scr, -jnp.inf)
        l_scr[...] = jnp.zeros_like(l_scr)
        acc_scr[...] = jnp.zeros_like(acc_scr)

    def step(diagonal):
        if diagonal:
            key = lax.broadcasted_iota(jnp.int32, (ks, blk), 0)
            qry = lax.broadcasted_iota(jnp.int32, (ks, blk), 1)
            hide = [jnp.where(qry >= key + u * ks, 0.0, -jnp.inf) for u in range(nsub)]
        for h in range(heads):
            cs = slice(h * LANE, (h + 1) * LANE)
            fq = fq_ref[0, h:h + 1, :]
            q = q_ref[:, cs]
            for u in range(nsub):
                rs = slice(u * ks, (u + 1) * ks)
                z = lax.dot_general(k_ref[rs, cs], q, NT,
                                    preferred_element_type=F32) * scale - fk_ref[rs, h:h + 1]
                if diagonal:
                    z = z + hide[u]
                m_prev = m_scr[h]
                m_new = jnp.maximum(m_prev, jnp.max(z, axis=0, keepdims=True) + fq)
                alpha = jnp.exp2(m_prev - m_new)
                p = jnp.exp2(z + (fq - m_new))
                l_scr[h] = alpha * l_scr[h] + jnp.sum(p, axis=0, keepdims=True)
                acc_scr[h] = alpha * acc_scr[h] + jnp.dot(
                    vt_ref[cs, rs], p.astype(BF16), preferred_element_type=F32)
                m_scr[h] = m_new

    @pl.when(j < i)
    def _():
        step(False)

    @pl.when(j == i)
    def _():
        step(True)
        for h in range(heads):
            cs = slice(h * LANE, (h + 1) * LANE)
            o_ref[:, cs] = (acc_scr[h] / l_scr[h]).T.astype(o_ref.dtype)


def _fox_attn_prompt(q, k, vt, m_total, nstream, t, heads, f_col, f_row):
    d_c = heads * LANE
    blk = _tile(t, ATTN_BLOCK, LANE)
    ks = _tile(blk, ATTN_KEY_SUB, LANE)
    nb = t // blk
    pairs = [(i, j) for i in range(nb) for j in range(i + 1)]
    qi = jnp.array([p[0] for p in pairs], jnp.int32)
    kj = jnp.array([p[1] for p in pairs], jnp.int32)
    q_row = lambda b, s, qi, kj: b * nb + qi[s]
    k_row = lambda b, s, qi, kj: b * nb + kj[s]
    stat = pltpu.VMEM((heads, 1, blk), F32)
    return pl.pallas_call(
        functools.partial(_fox_attn_kernel, heads=heads, scale=LANE ** -0.5 * LOG2E, blk=blk, ks=ks),
        grid_spec=pltpu.PrefetchScalarGridSpec(
            num_scalar_prefetch=2,
            grid=(nstream, len(pairs)),
            in_specs=[pl.BlockSpec((blk, d_c), lambda b, s, qi, kj: (q_row(b, s, qi, kj), 0)),
                      pl.BlockSpec((blk, d_c), lambda b, s, qi, kj: (k_row(b, s, qi, kj), 0)),
                      pl.BlockSpec((d_c, blk), lambda b, s, qi, kj: (0, k_row(b, s, qi, kj))),
                      pl.BlockSpec((1, heads, blk), lambda b, s, qi, kj: (b, 0, qi[s])),
                      pl.BlockSpec((blk, LANE), lambda b, s, qi, kj: (k_row(b, s, qi, kj), 0))],
            out_specs=pl.BlockSpec((blk, d_c), lambda b, s, qi, kj: (q_row(b, s, qi, kj), 0)),
            scratch_shapes=[stat, stat, pltpu.VMEM((heads, LANE, blk), F32)]),
        out_shape=jax.ShapeDtypeStruct((m_total, d_c), BF16),
        compiler_params=_params("parallel", "arbitrary"),
        name="fox_attn_prompt",
    )(qi, kj, q, k, vt, f_row, f_col)


def _fox_sample_kernel(q_ref, k_ref, v_ref, cf_ref, bias_ref, pk_ref, pv_ref, plf_ref,
                       o_ref, lf_ref, *, heads, scale):
    ts = q_ref.shape[0]
    past = pk_ref.shape[1]
    f_past = _row_cumsum(plf_ref[0])
    logf = _log_sigmoid(cf_ref[...] + bias_ref[...])
    lf_ref[0] = logf[:, :heads]
    f_new = f_past[past - 1:past, :] + _row_cumsum(logf)
    f_past_t = f_past.T
    f_new_t = jnp.concatenate([f_new, jnp.zeros((LANE - ts, LANE), F32)], axis=0).T
    tril = lax.broadcasted_iota(jnp.int32, (ts, ts), 0) >= lax.broadcasted_iota(jnp.int32, (ts, ts), 1)
    dotf = functools.partial(lax.dot_general, preferred_element_type=F32)
    for h in range(heads):
        cs = slice(h * LANE, (h + 1) * LANE)
        qh = q_ref[:, cs]
        fq = f_new[:, h:h + 1]
        s_p = dotf(qh, pk_ref[0, :, cs].astype(BF16), NT) * scale + fq - f_past_t[h:h + 1, :]
        s_n = dotf(qh, k_ref[:, cs], NT) * scale + fq - f_new_t[h:h + 1, :ts]
        s_n = jnp.where(tril, s_n, -jnp.inf)
        m = jnp.maximum(jnp.max(s_p, axis=-1, keepdims=True), jnp.max(s_n, axis=-1, keepdims=True))
        e_p = jnp.exp(s_p - m)
        e_n = jnp.exp(s_n - m)
        l = jnp.sum(e_p, axis=-1, keepdims=True) + jnp.sum(e_n, axis=-1, keepdims=True)
        acc = (jnp.dot(e_p.astype(BF16), pv_ref[0, :, cs].astype(BF16), preferred_element_type=F32)
               + jnp.dot(e_n.astype(BF16), v_ref[:, cs], preferred_element_type=F32))
        o_ref[:, cs] = (acc / l).astype(o_ref.dtype)


def _fox_attn_sample(q, row0, k_new, v_new, cf, bias, past_k, past_v, past_logf, l,
                     nstream, ts, heads, prev):
    d_c = heads * LANE
    past = past_k.shape[2]
    rb0 = row0 // ts
    new = pl.BlockSpec((ts, d_c), lambda b: (b, 0))
    cache = pl.BlockSpec((None, 1, past, d_c), lambda b: (l, b, 0, 0))
    return pl.pallas_call(
        _without_refs(functools.partial(_fox_sample_kernel, heads=heads, scale=LANE ** -0.5), 8, 1),
        grid=(nstream,),
        in_specs=[pl.BlockSpec((ts, d_c), lambda b: (rb0 + b, 0)), new, new,
                  pl.BlockSpec((ts, LANE), lambda b: (rb0 + b, 0)),
                  pl.BlockSpec((1, LANE), lambda b: (0, 0)),
                  cache, cache,
                  pl.BlockSpec((None, 1, past, LANE), lambda b: (l, b, 0, 0)),
                  ANY_SPEC],
        out_specs=[pl.BlockSpec((ts, d_c), lambda b: (rb0 + b, 0)),
                   pl.BlockSpec((1, ts, heads), lambda b: (b, 0, 0))],
        out_shape=[jax.ShapeDtypeStruct(prev.shape, BF16),
                   jax.ShapeDtypeStruct((nstream, ts, heads), F32)],
        input_output_aliases={8: 0},
        compiler_params=_params("parallel"),
        name="fox_attn_sample",
    )(q, k_new, v_new, cf, bias, past_k, past_v, past_logf, prev)


def kernel(x_prompt, x_sample, state_hgrn, cache_k, cache_v, cache_logf, ffn1_norm, ffn1_w_in,
           ffn1_w_out, mix_norm, w_in, w_gate, hgrn_lb_logits, hgrn_norm, gmlp_norm, gmlp_w_s,
           gmlp_b_s, fox_bias, w_branch_a, w_branch_b, w_branch_c, w_out, ffn2_norm, ffn2_w_in,
           ffn2_w_out, final_norm):
    nb, seq, d = x_prompt.shape
    ns, ts, _ = x_sample.shape
    depth = state_hgrn.shape[0]
    a_heads = state_hgrn.shape[2]
    c_heads = cache_k.shape[3]
    past = cache_k.shape[2]
    d_a = a_heads * LANE
    d_b = gmlp_norm.shape[1]
    d_c = c_heads * LANE
    col_b = 4 * d_a
    col_q = col_b + 2 * d_b
    col_k = col_q + d_c
    n_main = col_k + 2 * d_c
    mp = nb * seq
    ms = ns * ts
    assert w_in.shape[2] == n_main + c_heads and c_heads <= LANE
    assert col_b % d_b == 0 and mp % ts == 0

    bf = lambda w: w.astype(BF16)
    w1_in, w2_in = ffn1_w_in, ffn2_w_in
    w1_out, w2_out = bf(ffn1_w_out), bf(ffn2_w_out)
    w_in_b, w_gate_b, w_out_b = bf(w_in), bf(w_gate), bf(w_out)
    wa, wb, wc = bf(w_branch_a), bf(w_branch_b), bf(w_branch_c)
    w_cf = bf(jnp.pad(w_in[:, :, n_main:], ((0, 0), (0, 0), (0, LANE - c_heads))))
    bias_all = jnp.pad(fox_bias.astype(F32), ((0, 0), (0, LANE - c_heads)))
    past_k = cache_k.reshape(depth, ns, past, d_c)
    past_v = cache_v.reshape(depth, ns, past, d_c)
    past_lf = jnp.pad(cache_logf.astype(F32), ((0, 0), (0, 0), (0, 0), (0, LANE - c_heads)))

    lower_bounds = _lower_bounds(hgrn_lb_logits)
    x = jnp.concatenate([x_prompt.reshape(mp, d), x_sample.reshape(ms, d)], axis=0)
    s0_prompt = jnp.zeros((nb, a_heads, LANE, LANE), F32)

    kv_p = kv_s = None
    outs = {name: [] for name in ("sa_p", "sa_s", "vb_s", "lf_p", "lf_s")}
    for l in range(depth):
        bias = bias_all[l:l + 1]

        hid = _ffn_up(_rmsnorm(x, ffn1_norm[l], BF16), w1_in, l)
        h = _mm_res(hid, w1_out, l, x, 0.5, 256)

        n = _rmsnorm(h, mix_norm[l], BF16)
        p_ab = _mm(n, w_in_b, l, 0, col_q, F32)
        q_c = _mm(n, w_in_b, l, col_q, d_c, BF16)
        g = _mm(n, w_gate_b, l, 0, 3 * d, F32, gate=True)
        cf = _mm(n, w_cf, l, 0, LANE, F32)
        kp_all, vp_all, kp, vpt = _kv_proj(n, w_in_b, l, col_k, d_c, 0, mp, depth, kv_p, True)
        ks_all, vs_all, ks, vs = _kv_proj(n, w_in_b, l, col_k, d_c, mp, ms, depth, kv_s, False)
        kv_p, kv_s = (kp_all, vp_all), (ks_all, vs_all)

        oa, sa_p = _hgrn(p_ab, 0, nb, seq, a_heads, lower_bounds[l], hgrn_norm[l], s0_prompt, None)
        oa, sa_s = _hgrn(p_ab, mp, ns, ts, a_heads, lower_bounds[l], hgrn_norm[l], state_hgrn[l], oa)
        (ob,) = _gmlp(p_ab, 0, mp, seq, col_b, d_b, gmlp_norm[l], gmlp_w_s[l], gmlp_b_s[l], None)
        ob, vb_s = _gmlp(p_ab, mp, ms, ts, col_b, d_b, gmlp_norm[l], gmlp_w_s[l], gmlp_b_s[l], ob)
        lf_p, f_col, f_row = _fox_prep(cf, bias, nb, seq, c_heads)
        oc = _fox_attn_prompt(q_c, kp, vpt, mp + ms, nb, seq, c_heads, f_col, f_row)
        oc, lf_s = _fox_attn_sample(q_c, mp, ks, vs, cf, bias, past_k, past_v, past_lf, l,
                                    ns, ts, c_heads, oc)

        merged = _branch_merge(oa, ob, oc, wa, wb, wc, l, g)
        h = _mm_res(merged, w_out_b, l, h, 1.0, 1024)

        hid = _ffn_up(_rmsnorm(h, ffn2_norm[l], BF16), w2_in, l)
        x = _mm_res(hid, w2_out, l, h, 0.5, 256)

        outs["sa_p"].append(sa_p)
        outs["sa_s"].append(sa_s)
        outs["vb_s"].append(vb_s.reshape(ns, ts, d_b))
        outs["lf_p"].append(lf_p.reshape(nb, seq, c_heads))
        outs["lf_s"].append(lf_s)

    y_p = _rmsnorm(x, final_norm, F32, 0, mp)
    y_s = _rmsnorm(x, final_norm, F32, mp, ms)
    stack = lambda name: jnp.stack(outs[name])
    return (y_p.reshape(nb, seq, d), y_s.reshape(ns, ts, d),
            stack("sa_p"), stack("sa_s"), stack("vb_s"),
            kv_p[0].reshape(depth, nb, seq, c_heads, LANE), kv_s[0].reshape(depth, ns, ts, c_heads, LANE),
            kv_p[1].reshape(depth, nb, seq, c_heads, LANE), kv_s[1].reshape(depth, ns, ts, c_heads, LANE),
            stack("lf_p"), stack("lf_s"))
```

```python
import functools
import math

import jax
import jax.numpy as jnp
from jax import lax
from jax.experimental import pallas as pl
from jax.experimental.pallas import tpu as pltpu

EPS = 1e-6
LOG2E = 1.4426950408889634
LANE = 128
HGRN_CHUNK = 128
HGRN_ROWS_PER_STEP = 256
HGRN_HEADS_PER_STEP = 8
GMLP_CHUNK = 128
ATTN_BLOCK = 256
ATTN_KEY_SUB = 128
VMEM_LIMIT_BYTES = 56 * 1024 * 1024

F32 = jnp.float32
BF16 = jnp.bfloat16
NT = (((1,), (1,)), ((), ()))
TN = (((0,), (0,)), ((), ()))
ANY_SPEC = pl.BlockSpec(memory_space=pl.ANY)


def _tile(n, target, align):
    t = (min(target, n) // align) * align
    while t >= align:
        if n % t == 0:
            return t
        t -= align
    return n


def _params(*sem):
    return pltpu.CompilerParams(dimension_semantics=sem, vmem_limit_bytes=VMEM_LIMIT_BYTES)


def _sigmoid(x):
    return jax.nn.sigmoid(x)


def _row_cumsum(x):
    n = x.shape[0]
    row = lax.broadcasted_iota(jnp.int32, x.shape, 0)
    sh = 1
    while sh < n:
        x = x + jnp.where(row >= sh, pltpu.roll(x, sh, 0), 0.0)
        sh *= 2
    return x


def _without_refs(kernel, start, count):
    if count == 0:
        return kernel

    def wrapped(*refs):
        return kernel(*refs[:start], *refs[start + count:])
    return wrapped


def _rmsnorm_kernel(x_ref, g_ref, o_ref):
    x = x_ref[...]
    y = x * lax.rsqrt(jnp.mean(x * x, axis=-1, keepdims=True) + EPS)
    o_ref[...] = (y * g_ref[...]).astype(o_ref.dtype)


def _rmsnorm(x, g, out_dtype, row0=0, nrows=None):
    d = x.shape[1]
    nrows = x.shape[0] if nrows is None else nrows
    tm = _tile(math.gcd(row0, nrows), 256, 16)
    rb0 = row0 // tm
    return pl.pallas_call(
        _rmsnorm_kernel,
        grid=(nrows // tm,),
        in_specs=[pl.BlockSpec((tm, d), lambda i: (rb0 + i, 0)),
                  pl.BlockSpec((1, d), lambda i: (0, 0))],
        out_specs=pl.BlockSpec((tm, d), lambda i: (i, 0)),
        out_shape=jax.ShapeDtypeStruct((nrows, d), out_dtype),
        compiler_params=_params("parallel"),
        name="rmsnorm",
    )(x, g.reshape(1, d))


def _ffn_up_kernel(a_ref, wg_ref, wu_ref, o_ref):
    a = a_ref[...]
    gate = jnp.dot(a, wg_ref[...].astype(BF16), preferred_element_type=F32)
    up = jnp.dot(a, wu_ref[...].astype(BF16), preferred_element_type=F32)
    o_ref[...] = (gate * _sigmoid(gate) * up).astype(o_ref.dtype)


def _ffn_up(a, w_in, l):
    m, d = a.shape
    f = w_in.shape[2] // 2
    tm = _tile(m, 1536, 16)
    tn = _tile(f, 256, LANE)
    nf = f // tn
    return pl.pallas_call(
        _ffn_up_kernel,
        grid=(m // tm, nf),
        in_specs=[pl.BlockSpec((tm, d), lambda i, j: (i, 0)),
                  pl.BlockSpec((None, d, tn), lambda i, j: (l, 0, j)),
                  pl.BlockSpec((None, d, tn), lambda i, j: (l, 0, j + nf))],
        out_specs=pl.BlockSpec((tm, tn), lambda i, j: (i, j)),
        out_shape=jax.ShapeDtypeStruct((m, f), BF16),
        compiler_params=_params("parallel", "arbitrary"),
        name="ffn_up",
    )(a, w_in, w_in)


def _mm_res_kernel(a_ref, w_ref, r_ref, o_ref, *, scale):
    acc = jnp.dot(a_ref[...], w_ref[...], preferred_element_type=F32)
    o_ref[...] = r_ref[...] + scale * acc


def _mm_res(a, w, l, r, scale, tn_target):
    m, k = a.shape
    n = w.shape[2]
    tm = _tile(m, 768, 16)
    tn = _tile(n, tn_target, LANE)
    return pl.pallas_call(
        functools.partial(_mm_res_kernel, scale=scale),
        grid=(m // tm, n // tn),
        in_specs=[pl.BlockSpec((tm, k), lambda i, j: (i, 0)),
                  pl.BlockSpec((None, k, tn), lambda i, j: (l, 0, j)),
                  pl.BlockSpec((tm, tn), lambda i, j: (i, j))],
        out_specs=pl.BlockSpec((tm, tn), lambda i, j: (i, j)),
        out_shape=jax.ShapeDtypeStruct((m, n), F32),
        compiler_params=_params("parallel", "arbitrary"),
        name="mm_res",
    )(a, w, r)


def _mm_kernel(a_ref, w_ref, o_ref, *, gate, out_scale):
    acc = jnp.dot(a_ref[...], w_ref[...], preferred_element_type=F32)
    if out_scale != 1.0:
        acc = acc * out_scale
    o_ref[...] = (_sigmoid(acc) if gate else acc).astype(o_ref.dtype)


def _mm(a, w, l, col0, ncols, out_dtype, gate=False, out_scale=1.0, tn_target=1024):
    m, k = a.shape
    tm = _tile(m, 768, 16)
    tn = _tile(math.gcd(col0, ncols), tn_target, LANE)
    cb0 = col0 // tn
    return pl.pallas_call(
        functools.partial(_mm_kernel, gate=gate, out_scale=out_scale),
        grid=(m // tm, ncols // tn),
        in_specs=[pl.BlockSpec((tm, k), lambda i, j: (i, 0)),
                  pl.BlockSpec((None, k, tn), lambda i, j: (l, 0, cb0 + j))],
        out_specs=pl.BlockSpec((tm, tn), lambda i, j: (i, j)),
        out_shape=jax.ShapeDtypeStruct((m, ncols), out_dtype),
        compiler_params=_params("parallel", "arbitrary"),
        name="mm_gate" if gate else "mm",
    )(a, w)


def _kv_kernel(a_ref, wk_ref, wv_ref, kf_ref, vf_ref, kb_ref, vb_ref, *, transpose_v):
    a = a_ref[...]
    k = jnp.dot(a, wk_ref[...], preferred_element_type=F32)
    v = jnp.dot(a, wv_ref[...], preferred_element_type=F32)
    for hh in range(kf_ref.shape[1]):
        kf_ref[:, hh, :] = k[:, hh * LANE:(hh + 1) * LANE]
        vf_ref[:, hh, :] = v[:, hh * LANE:(hh + 1) * LANE]
    kb_ref[...] = k.astype(BF16)
    vb_ref[...] = (v.T if transpose_v else v).astype(BF16)


def _kv_proj(a, w, l, col_k, d_c, row0, nrows, depth, prev, transpose_v):
    d = a.shape[1]
    heads = d_c // LANE
    hpt = 8 if heads % 8 == 0 else heads
    tm = _tile(math.gcd(row0, nrows), 256, 16)
    tn = hpt * LANE
    rb0 = row0 // tm
    ck = col_k // tn
    cv = (col_k + d_c) // tn
    stacked = jax.ShapeDtypeStruct((depth, nrows, heads, LANE), F32)
    layer = jax.ShapeDtypeStruct((nrows, d_c), BF16)
    n_prev = 0 if prev is None else 2
    stack_spec = pl.BlockSpec((None, tm, hpt, LANE), lambda i, j: (l, i, j, 0))
    layer_spec = pl.BlockSpec((tm, tn), lambda i, j: (i, j))
    if transpose_v:
        v_shape = jax.ShapeDtypeStruct((d_c, nrows), BF16)
        v_spec = pl.BlockSpec((tn, tm), lambda i, j: (j, i))
    else:
        v_shape, v_spec = layer, layer_spec
    return pl.pallas_call(
        _without_refs(functools.partial(_kv_kernel, transpose_v=transpose_v), 3, n_prev),
        grid=(nrows // tm, d_c // tn),
        in_specs=[pl.BlockSpec((tm, d), lambda i, j: (rb0 + i, 0)),
                  pl.BlockSpec((None, d, tn), lambda i, j: (l, 0, ck + j)),
                  pl.BlockSpec((None, d, tn), lambda i, j: (l, 0, cv + j))] + [ANY_SPEC] * n_prev,
        out_specs=[stack_spec, stack_spec, layer_spec, v_spec],
        out_shape=[stacked, stacked, layer, v_shape],
        input_output_aliases={3: 0, 4: 1} if prev is not None else {},
        compiler_params=_params("parallel", "arbitrary"),
        name="kv_proj",
    )(a, w, w, *(prev or ()))


def _branch_kernel(oa_ref, ob_ref, oc_ref, wa_ref, wb_ref, wc_ref,
                   ga_ref, gb_ref, gc_ref, o_ref):
    dot = functools.partial(jnp.dot, preferred_element_type=F32)
    merged = (ga_ref[...] * dot(oa_ref[...], wa_ref[...])
              + gb_ref[...] * dot(ob_ref[...], wb_ref[...])
              + gc_ref[...] * dot(oc_ref[...], wc_ref[...]))
    o_ref[...] = merged.astype(o_ref.dtype)


def _branch_merge(oa, ob, oc, wa, wb, wc, l, g):
    m = oa.shape[0]
    d = wa.shape[2]
    tm = _tile(m, 768, 16)
    tn = _tile(d, 512, LANE)
    nd = d // tn
    a_spec = lambda x: pl.BlockSpec((tm, x.shape[1]), lambda i, j: (i, 0))
    w_spec = lambda x: pl.BlockSpec((None, x.shape[1], tn), lambda i, j: (l, 0, j))
    g_spec = lambda b: pl.BlockSpec((tm, tn), lambda i, j: (i, j + b * nd))
    return pl.pallas_call(
        _branch_kernel,
        grid=(m // tm, nd),
        in_specs=[a_spec(oa), a_spec(ob), a_spec(oc), w_spec(wa), w_spec(wb), w_spec(wc),
                  g_spec(0), g_spec(1), g_spec(2)],
        out_specs=pl.BlockSpec((tm, tn), lambda i, j: (i, j)),
        out_shape=jax.ShapeDtypeStruct((m, d), BF16),
        compiler_params=_params("parallel", "arbitrary"),
        name="branch_merge",
    )(oa, ob, oc, wa, wb, wc, g, g, g)


def _lower_bound_kernel(x_ref, o_ref):
    x = x_ref[...]
    depth = x.shape[0]
    e = jnp.exp(x - jnp.max(x, axis=0, keepdims=True))
    w = e / jnp.sum(e, axis=0, keepdims=True)
    run = w[0:1, :]
    rows = [run - w[0:1, :]]
    for l in range(1, depth):
        run = run + w[l:l + 1, :]
        rows.append(run - w[0:1, :])
    o_ref[...] = jnp.concatenate(rows, axis=0)


def _lower_bounds(logits):
    return pl.pallas_call(
        _lower_bound_kernel,
        out_shape=jax.ShapeDtypeStruct(logits.shape, F32),
        name="hgrn_lower_bounds",
    )(logits.astype(F32))


def _hgrn_kernel(q_ref, f_ref, i_ref, g_ref, lb_ref, ng_ref, s0_ref,
                 o_ref, sout_ref, st_scr, *, c, cpb, hpb):
    j = pl.program_id(2)

    @pl.when(j == 0)
    def _():
        for hh in range(hpb):
            st_scr[hh] = s0_ref[0, hh].T

    rowi = lax.broadcasted_iota(jnp.int32, (c, LANE), 0)
    ti = lax.broadcasted_iota(jnp.int32, (c, c), 0)
    si = lax.broadcasted_iota(jnp.int32, (c, c), 1)
    tx = ti ^ si
    eye = ti == si
    levels = []
    hb = 1
    while hb < c:
        levels.append(hb)
        hb *= 2
    masks = [(ti > si) & (tx >= hb) & (tx < 2 * hb) for hb in levels]
    dotf = functools.partial(lax.dot_general, preferred_element_type=F32)

    def level_ref(b, hb):
        n = 2 * hb
        if n == 2:
            return jnp.where((rowi & 1) != 0, pltpu.roll(b, 1, 0), b)
        if n == 4:
            m = rowi & 3
            return jnp.where(m == 0, pltpu.roll(b, c - 1, 0),
                             jnp.where(m == 1, b,
                                       jnp.where(m == 2, pltpu.roll(b, 1, 0),
                                                 pltpu.roll(b, 2, 0))))
        pieces = [jnp.broadcast_to(b[base + hb - 1:base + hb, :], (n, LANE))
                  for base in range(0, c, n)]
        return pieces[0] if len(pieces) == 1 else jnp.concatenate(pieces, axis=0)

    def head_chunk(rows, hh):
        cs = slice(hh * LANE, (hh + 1) * LANE)
        lb = lb_ref[:, cs]
        f = lb + (1.0 - lb) * _sigmoid(f_ref[rows, cs])
        kk = 1.0 - f
        qr = q_ref[rows, cs]
        q = qr * _sigmoid(qr)
        iv = i_ref[rows, cs].astype(BF16)
        b = _row_cumsum(jnp.log(f) * LOG2E)
        b_last = b[c - 1:c, :]
        st = st_scr[hh]
        o = dotf((q * jnp.exp2(b)).astype(BF16), st.astype(BF16), NT)
        a = jnp.where(eye, dotf(q.astype(BF16), kk.astype(BF16), NT), 0.0)
        for hb, mask in zip(levels, masks):
            upper = (rowi & hb) != 0
            d = b - level_ref(b, hb)
            x = (jnp.where(upper, q, kk) * jnp.exp2(jnp.where(upper, d, -d))).astype(BF16)
            a = a + jnp.where(mask, dotf(x, x, NT), 0.0)
        o = o + jnp.dot(a.astype(BF16), iv, preferred_element_type=F32)
        kb = kk * jnp.exp2(b_last - b)
        st_scr[hh] = st * jnp.exp2(b_last) + dotf(iv, kb.astype(BF16), TN)
        o = o * lax.rsqrt(jnp.mean(o * o, axis=-1, keepdims=True) + EPS) * ng_ref[:, cs]
        gr = g_ref[rows, cs]
        o_ref[rows, cs] = (o * (gr * _sigmoid(gr))).astype(o_ref.dtype)

    for ci in range(cpb):
        for hh in range(hpb):
            head_chunk(slice(ci * c, (ci + 1) * c), hh)

    @pl.when(j == pl.num_programs(2) - 1)
    def _():
        for hh in range(hpb):
            sout_ref[0, hh] = st_scr[hh].T


def _hgrn(p, row0, nstream, t, heads, lb, norm_g, s0, prev):
    m = p.shape[0]
    c = min(t, HGRN_CHUNK)
    rows = _tile(t, HGRN_ROWS_PER_STEP, c)
    cpb = rows // c
    nj = t // rows
    rb0 = row0 // rows
    hpb = _tile(heads, HGRN_HEADS_PER_STEP, 1)
    ng = heads // hpb
    w = hpb * LANE
    col = lambda part: (lambda b, hg, j: (rb0 + b * nj + j, part * ng + hg))
    vec = pl.BlockSpec((1, w), lambda b, hg, j: (0, hg))
    st_spec = pl.BlockSpec((1, hpb, LANE, LANE), lambda b, hg, j: (b, hg, 0, 0))
    n_prev = 0 if prev is None else 1
    o, s_out = pl.pallas_call(
        _without_refs(functools.partial(_hgrn_kernel, c=c, cpb=cpb, hpb=hpb), 7, n_prev),
        grid=(nstream, ng, nj),
        in_specs=[pl.BlockSpec((rows, w), col(0)), pl.BlockSpec((rows, w), col(1)),
                  pl.BlockSpec((rows, w), col(2)), pl.BlockSpec((rows, w), col(3)),
                  vec, vec, st_spec] + [ANY_SPEC] * n_prev,
        out_specs=[pl.BlockSpec((rows, w), lambda b, hg, j: (rb0 + b * nj + j, hg)), st_spec],
        out_shape=[jax.ShapeDtypeStruct((m, heads * LANE), BF16),
                   jax.ShapeDtypeStruct((nstream, heads, LANE, LANE), F32)],
        scratch_shapes=[pltpu.VMEM((hpb, LANE, LANE), F32)],
        input_output_aliases={7: 0} if prev is not None else {},
        compiler_params=_params("parallel", "parallel", "arbitrary"),
        name="hgrn2",
    )(p, p, p, p, lb.reshape(1, -1), norm_g.reshape(1, -1), s0, *(() if prev is None else (prev,)))
    return o, s_out


def _gmlp_kernel(u_ref, v_ref, ng_ref, w_ref, bst_ref, o_ref, *vout_ref, groups):
    v = v_ref[...]
    vn = v * lax.rsqrt(jnp.mean(v * v, axis=-1, keepdims=True) + EPS) * ng_ref[...]
    if vout_ref:
        vout_ref[0][...] = vn
    c = v.shape[0]
    tril = lax.broadcasted_iota(jnp.int32, (c, c), 0) >= lax.broadcasted_iota(jnp.int32, (c, c), 1)
    for g in range(groups):
        cs = slice(g * LANE, (g + 1) * LANE)
        w = jnp.where(tril, w_ref[g], 0.0).astype(BF16)
        s = jnp.dot(w, vn[:, cs].astype(BF16), preferred_element_type=F32) + bst_ref[:, g:g + 1]
        o_ref[:, cs] = (u_ref[:, cs] * s).astype(o_ref.dtype)


def _gmlp(p, row0, nrows, t, col0, d_b, norm_g, w_s, b_s, prev):
    m = p.shape[0]
    groups = w_s.shape[0]
    c = min(t, GMLP_CHUNK)
    w = w_s[:, :c, :c]
    bst = jnp.transpose(b_s[:, :c])
    rb0 = row0 // c
    cb = col0 // d_b
    emit_v = prev is not None
    out_shape = [jax.ShapeDtypeStruct((m, d_b), BF16)]
    out_specs = [pl.BlockSpec((c, d_b), lambda i: (rb0 + i, 0))]
    if emit_v:
        out_shape.append(jax.ShapeDtypeStruct((nrows, d_b), F32))
        out_specs.append(pl.BlockSpec((c, d_b), lambda i: (i, 0)))
    n_prev = 0 if prev is None else 1
    return pl.pallas_call(
        _without_refs(functools.partial(_gmlp_kernel, groups=groups), 5, n_prev),
        grid=(nrows // c,),
        in_specs=[pl.BlockSpec((c, d_b), lambda i: (rb0 + i, cb)),
                  pl.BlockSpec((c, d_b), lambda i: (rb0 + i, cb + 1)),
                  pl.BlockSpec((1, d_b), lambda i: (0, 0)),
                  pl.BlockSpec((groups, c, c), lambda i: (0, 0, 0)),
                  pl.BlockSpec((c, groups), lambda i: (0, 0))] + [ANY_SPEC] * n_prev,
        out_specs=out_specs,
        out_shape=out_shape,
        input_output_aliases={5: 0} if prev is not None else {},
        compiler_params=_params("parallel"),
        name="gmlp",
    )(p, p, norm_g.reshape(1, d_b), w, bst, *(() if prev is None else (prev,)))


def _log_sigmoid(z):
    return jnp.minimum(z, 0.0) - jnp.log1p(jnp.exp(-jnp.abs(z)))


def _fox_prep_kernel(cf_ref, bias_ref, logf_ref, f_ref, ft_ref, carry, *, heads):
    @pl.when(pl.program_id(1) == 0)
    def _():
        carry[...] = jnp.zeros_like(carry)

    logf = _log_sigmoid(cf_ref[...] + bias_ref[...])
    logf_ref[...] = logf[:, :heads]
    fc = carry[...] + _row_cumsum(logf)
    carry[...] = fc[fc.shape[0] - 1:, :]
    fc = fc * LOG2E
    f_ref[...] = fc
    ft_ref[0] = fc.T[:heads, :]


def _fox_prep(cf, bias, nstream, t, heads):
    tb = _tile(t, 256, LANE)
    nj = t // tb
    return pl.pallas_call(
        functools.partial(_fox_prep_kernel, heads=heads),
        grid=(nstream, nj),
        in_specs=[pl.BlockSpec((tb, LANE), lambda b, j: (b * nj + j, 0)),
                  pl.BlockSpec((1, LANE), lambda b, j: (0, 0))],
        out_specs=[pl.BlockSpec((tb, heads), lambda b, j: (b * nj + j, 0)),
                   pl.BlockSpec((tb, LANE), lambda b, j: (b * nj + j, 0)),
                   pl.BlockSpec((1, heads, tb), lambda b, j: (b, 0, j))],
        out_shape=[jax.ShapeDtypeStruct((nstream * t, heads), F32),
                   jax.ShapeDtypeStruct((nstream * t, LANE), F32),
                   jax.ShapeDtypeStruct((nstream, heads, t), F32)],
        scratch_shapes=[pltpu.VMEM((1, LANE), F32)],
        compiler_params=_params("parallel", "arbitrary"),
        name="fox_prep",
    )(cf, bias)


def _fox_attn_kernel(qi_ref, kj_ref, q_ref, k_ref, vt_ref, fq_ref, fk_ref, o_ref,
                     m_scr, l_scr, acc_scr, *, heads, blk, ks):
    t = pl.program_id(1)
    i = qi_ref[t]
    j = kj_ref[t]
    nsub = blk // ks

    @pl.when(j == 0)
    def _():
        m_scr[...] = jnp.full_like(m_scr, -jnp.inf)
        l_scr[...] = jnp.zeros_like(l_scr)
        acc_scr[...] = jnp.zeros_like(acc_scr)

    def step(diagonal):
        if diagonal:
            key = lax.broadcasted_iota(jnp.int32, (ks, blk), 0)
            qry = lax.broadcasted_iota(jnp.int32, (ks, blk), 1)
            hide = [jnp.where(qry >= key + u * ks, 0.0, -jnp.inf) for u in range(nsub)]
        for h in range(heads):
            cs = slice(h * LANE, (h + 1) * LANE)
            fq = fq_ref[0, h:h + 1, :]
            q = q_ref[:, cs]
            for u in range(nsub):
                rs = slice(u * ks, (u + 1) * ks)
                z = lax.dot_general(k_ref[rs, cs], q, NT,
                                    preferred_element_type=F32) - fk_ref[rs, h:h + 1]
                if diagonal:
                    z = z + hide[u]
                m_prev = m_scr[h]
                m_new = jnp.maximum(m_prev, jnp.max(z, axis=0, keepdims=True) + fq)
                alpha = jnp.exp2(m_prev - m_new)
                p = jnp.exp2(z + (fq - m_new))
                l_scr[h] = alpha * l_scr[h] + jnp.sum(p, axis=0, keepdims=True)
                acc_scr[h] = alpha * acc_scr[h] + jnp.dot(
                    vt_ref[cs, rs], p.astype(BF16), preferred_element_type=F32)
                m_scr[h] = m_new

    @pl.when(j < i)
    def _():
        step(False)

    @pl.when(j == i)
    def _():
        step(True)
        for h in range(heads):
            cs = slice(h * LANE, (h + 1) * LANE)
            o_ref[:, cs] = (acc_scr[h] / l_scr[h]).T.astype(o_ref.dtype)


def _fox_attn_prompt(q, k, vt, m_total, nstream, t, heads, f_col, f_row):
    d_c = heads * LANE
    blk = _tile(t, ATTN_BLOCK, LANE)
    ks = _tile(blk, ATTN_KEY_SUB, LANE)
    nb = t // blk
    pairs = [(i, j) for i in range(nb) for j in range(i + 1)]
    qi = jnp.array([p[0] for p in pairs], jnp.int32)
    kj = jnp.array([p[1] for p in pairs], jnp.int32)
    q_row = lambda b, s, qi, kj: b * nb + qi[s]
    k_row = lambda b, s, qi, kj: b * nb + kj[s]
    stat = pltpu.VMEM((heads, 1, blk), F32)
    return pl.pallas_call(
        functools.partial(_fox_attn_kernel, heads=heads, blk=blk, ks=ks),
        grid_spec=pltpu.PrefetchScalarGridSpec(
            num_scalar_prefetch=2,
            grid=(nstream, len(pairs)),
            in_specs=[pl.BlockSpec((blk, d_c), lambda b, s, qi, kj: (q_row(b, s, qi, kj), 0)),
                      pl.BlockSpec((blk, d_c), lambda b, s, qi, kj: (k_row(b, s, qi, kj), 0)),
                      pl.BlockSpec((d_c, blk), lambda b, s, qi, kj: (0, k_row(b, s, qi, kj))),
                      pl.BlockSpec((1, heads, blk), lambda b, s, qi, kj: (b, 0, qi[s])),
                      pl.BlockSpec((blk, LANE), lambda b, s, qi, kj: (k_row(b, s, qi, kj), 0))],
            out_specs=pl.BlockSpec((blk, d_c), lambda b, s, qi, kj: (q_row(b, s, qi, kj), 0)),
            scratch_shapes=[stat, stat, pltpu.VMEM((heads, LANE, blk), F32)]),
        out_shape=jax.ShapeDtypeStruct((m_total, d_c), BF16),
        compiler_params=_params("parallel", "arbitrary"),
        name="fox_attn_prompt",
    )(qi, kj, q, k, vt, f_row, f_col)


def _fox_sample_kernel(q_ref, k_ref, v_ref, cf_ref, bias_ref, pk_ref, pv_ref, plf_ref,
                       o_ref, lf_ref, *, heads):
    ts = q_ref.shape[0]
    past = pk_ref.shape[1]
    f_past = _row_cumsum(plf_ref[0])
    logf = _log_sigmoid(cf_ref[...] + bias_ref[...])
    lf_ref[0] = logf[:, :heads]
    f_new = f_past[past - 1:past, :] + _row_cumsum(logf)
    f_past = f_past * LOG2E
    f_new = f_new * LOG2E
    f_past_t = f_past.T
    f_new_t = jnp.concatenate([f_new, jnp.zeros((LANE - ts, LANE), F32)], axis=0).T
    tril = lax.broadcasted_iota(jnp.int32, (ts, ts), 0) >= lax.broadcasted_iota(jnp.int32, (ts, ts), 1)
    dotf = functools.partial(lax.dot_general, preferred_element_type=F32)
    for h in range(heads):
        cs = slice(h * LANE, (h + 1) * LANE)
        qh = q_ref[:, cs]
        fq = f_new[:, h:h + 1]
        s_p = dotf(qh, pk_ref[0, :, cs].astype(BF16), NT) + fq - f_past_t[h:h + 1, :]
        s_n = dotf(qh, k_ref[:, cs], NT) + fq - f_new_t[h:h + 1, :ts]
        s_n = jnp.where(tril, s_n, -jnp.inf)
        m = jnp.maximum(jnp.max(s_p, axis=-1, keepdims=True), jnp.max(s_n, axis=-1, keepdims=True))
        e_p = jnp.exp2(s_p - m)
        e_n = jnp.exp2(s_n - m)
        l = jnp.sum(e_p, axis=-1, keepdims=True) + jnp.sum(e_n, axis=-1, keepdims=True)
        acc = (jnp.dot(e_p.astype(BF16), pv_ref[0, :, cs].astype(BF16), preferred_element_type=F32)
               + jnp.dot(e_n.astype(BF16), v_ref[:, cs], preferred_element_type=F32))
        o_ref[:, cs] = (acc / l).astype(o_ref.dtype)


def _fox_attn_sample(q, row0, k_new, v_new, cf, bias, past_k, past_v, past_logf, l,
                     nstream, ts, heads, prev):
    d_c = heads * LANE
    past = past_k.shape[2]
    rb0 = row0 // ts
    new = pl.BlockSpec((ts, d_c), lambda b: (b, 0))
    cache = pl.BlockSpec((None, 1, past, d_c), lambda b: (l, b, 0, 0))
    return pl.pallas_call(
        _without_refs(functools.partial(_fox_sample_kernel, heads=heads), 8, 1),
        grid=(nstream,),
        in_specs=[pl.BlockSpec((ts, d_c), lambda b: (rb0 + b, 0)), new, new,
                  pl.BlockSpec((ts, LANE), lambda b: (rb0 + b, 0)),
                  pl.BlockSpec((1, LANE), lambda b: (0, 0)),
                  cache, cache,
                  pl.BlockSpec((None, 1, past, LANE), lambda b: (l, b, 0, 0)),
                  ANY_SPEC],
        out_specs=[pl.BlockSpec((ts, d_c), lambda b: (rb0 + b, 0)),
                   pl.BlockSpec((1, ts, heads), lambda b: (b, 0, 0))],
        out_shape=[jax.ShapeDtypeStruct(prev.shape, BF16),
                   jax.ShapeDtypeStruct((nstream, ts, heads), F32)],
        input_output_aliases={8: 0},
        compiler_params=_params("parallel"),
        name="fox_attn_sample",
    )(q, k_new, v_new, cf, bias, past_k, past_v, past_logf, prev)


def kernel(x_prompt, x_sample, state_hgrn, cache_k, cache_v, cache_logf, ffn1_norm, ffn1_w_in,
           ffn1_w_out, mix_norm, w_in, w_gate, hgrn_lb_logits, hgrn_norm, gmlp_norm, gmlp_w_s,
           gmlp_b_s, fox_bias, w_branch_a, w_branch_b, w_branch_c, w_out, ffn2_norm, ffn2_w_in,
           ffn2_w_out, final_norm):
    nb, seq, d = x_prompt.shape
    ns, ts, _ = x_sample.shape
    depth = state_hgrn.shape[0]
    a_heads = state_hgrn.shape[2]
    c_heads = cache_k.shape[3]
    past = cache_k.shape[2]
    d_a = a_heads * LANE
    d_b = gmlp_norm.shape[1]
    d_c = c_heads * LANE
    col_b = 4 * d_a
    col_q = col_b + 2 * d_b
    col_k = col_q + d_c
    n_main = col_k + 2 * d_c
    mp = nb * seq
    ms = ns * ts
    assert w_in.shape[2] == n_main + c_heads and c_heads <= LANE
    assert col_b % d_b == 0 and mp % ts == 0

    bf = lambda w: w.astype(BF16)
    w1_in, w2_in = ffn1_w_in, ffn2_w_in
    w1_out, w2_out = bf(ffn1_w_out), bf(ffn2_w_out)
    w_in_b, w_gate_b, w_out_b = bf(w_in), bf(w_gate), bf(w_out)
    wa, wb, wc = bf(w_branch_a), bf(w_branch_b), bf(w_branch_c)
    w_cf = bf(jnp.pad(w_in[:, :, n_main:], ((0, 0), (0, 0), (0, LANE - c_heads))))
    bias_all = jnp.pad(fox_bias.astype(F32), ((0, 0), (0, LANE - c_heads)))
    past_k = cache_k.reshape(depth, ns, past, d_c)
    past_v = cache_v.reshape(depth, ns, past, d_c)
    past_lf = jnp.pad(cache_logf.astype(F32), ((0, 0), (0, 0), (0, 0), (0, LANE - c_heads)))

    lower_bounds = _lower_bounds(hgrn_lb_logits)
    x = jnp.concatenate([x_prompt.reshape(mp, d), x_sample.reshape(ms, d)], axis=0)
    s0_prompt = jnp.zeros((nb, a_heads, LANE, LANE), F32)

    kv_p = kv_s = None
    outs = {name: [] for name in ("sa_p", "sa_s", "vb_s", "lf_p", "lf_s")}
    for l in range(depth):
        bias = bias_all[l:l + 1]

        hid = _ffn_up(_rmsnorm(x, ffn1_norm[l], BF16), w1_in, l)
        h = _mm_res(hid, w1_out, l, x, 0.5, 256)

        n = _rmsnorm(h, mix_norm[l], BF16)
        p_ab = _mm(n, w_in_b, l, 0, col_q, F32)
        q_c = _mm(n, w_in_b, l, col_q, d_c, BF16, out_scale=LANE ** -0.5 * LOG2E)
        g = _mm(n, w_gate_b, l, 0, 3 * d, F32, gate=True)
        cf = _mm(n, w_cf, l, 0, LANE, F32)
        kp_all, vp_all, kp, vpt = _kv_proj(n, w_in_b, l, col_k, d_c, 0, mp, depth, kv_p, True)
        ks_all, vs_all, ks, vs = _kv_proj(n, w_in_b, l, col_k, d_c, mp, ms, depth, kv_s, False)
        kv_p, kv_s = (kp_all, vp_all), (ks_all, vs_all)

        oa, sa_p = _hgrn(p_ab, 0, nb, seq, a_heads, lower_bounds[l], hgrn_norm[l], s0_prompt, None)
        oa, sa_s = _hgrn(p_ab, mp, ns, ts, a_heads, lower_bounds[l], hgrn_norm[l], state_hgrn[l], oa)
        (ob,) = _gmlp(p_ab, 0, mp, seq, col_b, d_b, gmlp_norm[l], gmlp_w_s[l], gmlp_b_s[l], None)
        ob, vb_s = _gmlp(p_ab, mp, ms, ts, col_b, d_b, gmlp_norm[l], gmlp_w_s[l], gmlp_b_s[l], ob)
        lf_p, f_col, f_row = _fox_prep(cf, bias, nb, seq, c_heads)
        oc = _fox_attn_prompt(q_c, kp, vpt, mp + ms, nb, seq, c_heads, f_col, f_row)
        oc, lf_s = _fox_attn_sample(q_c, mp, ks, vs, cf, bias, past_k, past_v, past_lf, l,
                                    ns, ts, c_heads, oc)

        merged = _branch_merge(oa, ob, oc, wa, wb, wc, l, g)
        h = _mm_res(merged, w_out_b, l, h, 1.0, 1024)

        hid = _ffn_up(_rmsnorm(h, ffn2_norm[l], BF16), w2_in, l)
        x = _mm_res(hid, w2_out, l, h, 0.5, 256)

        outs["sa_p"].append(sa_p)
        outs["sa_s"].append(sa_s)
        outs["vb_s"].append(vb_s.reshape(ns, ts, d_b))
        outs["lf_p"].append(lf_p.reshape(nb, seq, c_heads))
        outs["lf_s"].append(lf_s)

    y_p = _rmsnorm(x, final_norm, F32, 0, mp)
    y_s = _rmsnorm(x, final_norm, F32, mp, ms)
    stack = lambda name: jnp.stack(outs[name])
    return (y_p.reshape(nb, seq, d), y_s.reshape(ns, ts, d),
            stack("sa_p"), stack("sa_s"), stack("vb_s"),
            kv_p[0].reshape(depth, nb, seq, c_heads, LANE), kv_s[0].reshape(depth, ns, ts, c_heads, LANE),
            kv_p[1].reshape(depth, nb, seq, c_heads, LANE), kv_s[1].reshape(depth, ns, ts, c_heads, LANE),
            stack("lf_p"), stack("lf_s"))
```

```python
import functools
import math

import jax
import jax.numpy as jnp
from jax import lax
from jax.experimental import pallas as pl
from jax.experimental.pallas import tpu as pltpu

EPS = 1e-6
LOG2E = 1.4426950408889634
LANE = 128
HGRN_CHUNK = 128
HGRN_ROWS_PER_STEP = 256
HGRN_HEADS_PER_STEP = 8
GMLP_CHUNK = 128
ATTN_BLOCK = 256
ATTN_KEY_SUB = 128
VMEM_LIMIT_BYTES = 56 * 1024 * 1024

F32 = jnp.float32
BF16 = jnp.bfloat16
NT = (((1,), (1,)), ((), ()))
TN = (((0,), (0,)), ((), ()))
ANY_SPEC = pl.BlockSpec(memory_space=pl.ANY)


def _tile(n, target, align):
    t = (min(target, n) // align) * align
    while t >= align:
        if n % t == 0:
            return t
        t -= align
    return n


def _params(*sem):
    return pltpu.CompilerParams(dimension_semantics=sem, vmem_limit_bytes=VMEM_LIMIT_BYTES)


def _sigmoid(x):
    return jax.nn.sigmoid(x)


def _row_cumsum(x):
    n = x.shape[0]
    row = lax.broadcasted_iota(jnp.int32, x.shape, 0)
    sh = 1
    while sh < n:
        x = x + jnp.where(row >= sh, pltpu.roll(x, sh, 0), 0.0)
        sh *= 2
    return x


def _without_refs(kernel, start, count):
    if count == 0:
        return kernel

    def wrapped(*refs):
        return kernel(*refs[:start], *refs[start + count:])
    return wrapped


def _rmsnorm_kernel(x_ref, g_ref, o_ref):
    x = x_ref[...]
    y = x * lax.rsqrt(jnp.mean(x * x, axis=-1, keepdims=True) + EPS)
    o_ref[...] = (y * g_ref[...]).astype(o_ref.dtype)


def _rmsnorm(x, g, out_dtype, row0=0, nrows=None):
    d = x.shape[1]
    nrows = x.shape[0] if nrows is None else nrows
    tm = _tile(math.gcd(row0, nrows), 256, 16)
    rb0 = row0 // tm
    return pl.pallas_call(
        _rmsnorm_kernel,
        grid=(nrows // tm,),
        in_specs=[pl.BlockSpec((tm, d), lambda i: (rb0 + i, 0)),
                  pl.BlockSpec((1, d), lambda i: (0, 0))],
        out_specs=pl.BlockSpec((tm, d), lambda i: (i, 0)),
        out_shape=jax.ShapeDtypeStruct((nrows, d), out_dtype),
        compiler_params=_params("parallel"),
        name="rmsnorm",
    )(x, g.reshape(1, d))


def _ffn_up_kernel(a_ref, wg_ref, wu_ref, o_ref):
    a = a_ref[...]
    gate = jnp.dot(a, wg_ref[...].astype(BF16), preferred_element_type=F32)
    up = jnp.dot(a, wu_ref[...].astype(BF16), preferred_element_type=F32)
    o_ref[...] = (gate * _sigmoid(gate) * up).astype(o_ref.dtype)


def _ffn_up(a, w_in, l):
    m, d = a.shape
    f = w_in.shape[2] // 2
    tm = _tile(m, 1536, 16)
    tn = _tile(f, 256, LANE)
    nf = f // tn
    return pl.pallas_call(
        _ffn_up_kernel,
        grid=(m // tm, nf),
        in_specs=[pl.BlockSpec((tm, d), lambda i, j: (i, 0)),
                  pl.BlockSpec((None, d, tn), lambda i, j: (l, 0, j)),
                  pl.BlockSpec((None, d, tn), lambda i, j: (l, 0, j + nf))],
        out_specs=pl.BlockSpec((tm, tn), lambda i, j: (i, j)),
        out_shape=jax.ShapeDtypeStruct((m, f), BF16),
        compiler_params=_params("parallel", "arbitrary"),
        name="ffn_up",
    )(a, w_in, w_in)


def _mm_res_kernel(a_ref, w_ref, r_ref, o_ref, *, scale):
    acc = jnp.dot(a_ref[...], w_ref[...], preferred_element_type=F32)
    o_ref[...] = r_ref[...] + scale * acc


def _mm_res(a, w, l, r, scale, tn_target):
    m, k = a.shape
    n = w.shape[2]
    tm = _tile(m, 768, 16)
    tn = _tile(n, tn_target, LANE)
    return pl.pallas_call(
        functools.partial(_mm_res_kernel, scale=scale),
        grid=(m // tm, n // tn),
        in_specs=[pl.BlockSpec((tm, k), lambda i, j: (i, 0)),
                  pl.BlockSpec((None, k, tn), lambda i, j: (l, 0, j)),
                  pl.BlockSpec((tm, tn), lambda i, j: (i, j))],
        out_specs=pl.BlockSpec((tm, tn), lambda i, j: (i, j)),
        out_shape=jax.ShapeDtypeStruct((m, n), F32),
        compiler_params=_params("parallel", "arbitrary"),
        name="mm_res",
    )(a, w, r)


def _mm_kernel(a_ref, w_ref, o_ref, *, gate, out_scale):
    acc = jnp.dot(a_ref[...], w_ref[...], preferred_element_type=F32)
    if out_scale != 1.0:
        acc = acc * out_scale
    o_ref[...] = (_sigmoid(acc) if gate else acc).astype(o_ref.dtype)


def _mm(a, w, l, col0, ncols, out_dtype, gate=False, out_scale=1.0, tn_target=1024):
    m, k = a.shape
    tm = _tile(m, 768, 16)
    tn = _tile(math.gcd(col0, ncols), tn_target, LANE)
    cb0 = col0 // tn
    return pl.pallas_call(
        functools.partial(_mm_kernel, gate=gate, out_scale=out_scale),
        grid=(m // tm, ncols // tn),
        in_specs=[pl.BlockSpec((tm, k), lambda i, j: (i, 0)),
                  pl.BlockSpec((None, k, tn), lambda i, j: (l, 0, cb0 + j))],
        out_specs=pl.BlockSpec((tm, tn), lambda i, j: (i, j)),
        out_shape=jax.ShapeDtypeStruct((m, ncols), out_dtype),
        compiler_params=_params("parallel", "arbitrary"),
        name="mm_gate" if gate else "mm",
    )(a, w)


def _kv_kernel(a_ref, wk_ref, wv_ref, kf_ref, vf_ref, kb_ref, vb_ref, *, transpose_v):
    a = a_ref[...]
    k = jnp.dot(a, wk_ref[...], preferred_element_type=F32)
    v = jnp.dot(a, wv_ref[...], preferred_element_type=F32)
    kf_ref[...] = k
    vf_ref[...] = v
    kb_ref[...] = k.astype(BF16)
    vb_ref[...] = (v.T if transpose_v else v).astype(BF16)


def _kv_proj(a, w, l, col_k, d_c, row0, nrows, depth, prev, transpose_v):
    d = a.shape[1]
    tm = _tile(math.gcd(row0, nrows), 512, 16)
    tn = _tile(d_c, 512, LANE)
    rb0 = row0 // tm
    ck = col_k // tn
    cv = (col_k + d_c) // tn
    stacked = jax.ShapeDtypeStruct((depth, nrows, d_c), F32)
    layer = jax.ShapeDtypeStruct((nrows, d_c), BF16)
    n_prev = 0 if prev is None else 2
    stack_spec = pl.BlockSpec((None, tm, tn), lambda i, j: (l, i, j))
    layer_spec = pl.BlockSpec((tm, tn), lambda i, j: (i, j))
    if transpose_v:
        v_shape = jax.ShapeDtypeStruct((d_c, nrows), BF16)
        v_spec = pl.BlockSpec((tn, tm), lambda i, j: (j, i))
    else:
        v_shape, v_spec = layer, layer_spec
    return pl.pallas_call(
        _without_refs(functools.partial(_kv_kernel, transpose_v=transpose_v), 3, n_prev),
        grid=(nrows // tm, d_c // tn),
        in_specs=[pl.BlockSpec((tm, d), lambda i, j: (rb0 + i, 0)),
                  pl.BlockSpec((None, d, tn), lambda i, j: (l, 0, ck + j)),
                  pl.BlockSpec((None, d, tn), lambda i, j: (l, 0, cv + j))] + [ANY_SPEC] * n_prev,
        out_specs=[stack_spec, stack_spec, layer_spec, v_spec],
        out_shape=[stacked, stacked, layer, v_shape],
        input_output_aliases={3: 0, 4: 1} if prev is not None else {},
        compiler_params=_params("parallel", "arbitrary"),
        name="kv_proj",
    )(a, w, w, *(prev or ()))


def _branch_kernel(oa_ref, ob_ref, oc_ref, wa_ref, wb_ref, wc_ref,
                   ga_ref, gb_ref, gc_ref, o_ref):
    dot = functools.partial(jnp.dot, preferred_element_type=F32)
    merged = (ga_ref[...] * dot(oa_ref[...], wa_ref[...])
              + gb_ref[...] * dot(ob_ref[...], wb_ref[...])
              + gc_ref[...] * dot(oc_ref[...], wc_ref[...]))
    o_ref[...] = merged.astype(o_ref.dtype)


def _branch_merge(oa, ob, oc, wa, wb, wc, l, g):
    m = oa.shape[0]
    d = wa.shape[2]
    tm = _tile(m, 768, 16)
    tn = _tile(d, 512, LANE)
    nd = d // tn
    a_spec = lambda x: pl.BlockSpec((tm, x.shape[1]), lambda i, j: (i, 0))
    w_spec = lambda x: pl.BlockSpec((None, x.shape[1], tn), lambda i, j: (l, 0, j))
    g_spec = lambda b: pl.BlockSpec((tm, tn), lambda i, j: (i, j + b * nd))
    return pl.pallas_call(
        _branch_kernel,
        grid=(m // tm, nd),
        in_specs=[a_spec(oa), a_spec(ob), a_spec(oc), w_spec(wa), w_spec(wb), w_spec(wc),
                  g_spec(0), g_spec(1), g_spec(2)],
        out_specs=pl.BlockSpec((tm, tn), lambda i, j: (i, j)),
        out_shape=jax.ShapeDtypeStruct((m, d), BF16),
        compiler_params=_params("parallel", "arbitrary"),
        name="branch_merge",
    )(oa, ob, oc, wa, wb, wc, g, g, g)


def _lower_bound_kernel(x_ref, o_ref):
    x = x_ref[...]
    depth = x.shape[0]
    e = jnp.exp(x - jnp.max(x, axis=0, keepdims=True))
    w = e / jnp.sum(e, axis=0, keepdims=True)
    run = w[0:1, :]
    rows = [run - w[0:1, :]]
    for l in range(1, depth):
        run = run + w[l:l + 1, :]
        rows.append(run - w[0:1, :])
    o_ref[...] = jnp.concatenate(rows, axis=0)


def _lower_bounds(logits):
    return pl.pallas_call(
        _lower_bound_kernel,
        out_shape=jax.ShapeDtypeStruct(logits.shape, F32),
        name="hgrn_lower_bounds",
    )(logits.astype(F32))


def _hgrn_kernel(q_ref, f_ref, i_ref, g_ref, lb_ref, ng_ref, s0_ref,
                 o_ref, sout_ref, st_scr, *, c, cpb, hpb):
    j = pl.program_id(2)

    @pl.when(j == 0)
    def _():
        for hh in range(hpb):
            st_scr[hh] = s0_ref[0, hh].T

    rowi = lax.broadcasted_iota(jnp.int32, (c, LANE), 0)
    ti = lax.broadcasted_iota(jnp.int32, (c, c), 0)
    si = lax.broadcasted_iota(jnp.int32, (c, c), 1)
    tx = ti ^ si
    eye = ti == si
    levels = []
    hb = 1
    while hb < c:
        levels.append(hb)
        hb *= 2
    masks = [(ti > si) & (tx >= hb) & (tx < 2 * hb) for hb in levels]
    dotf = functools.partial(lax.dot_general, preferred_element_type=F32)

    def level_ref(b, hb):
        n = 2 * hb
        if n == 2:
            return jnp.where((rowi & 1) != 0, pltpu.roll(b, 1, 0), b)
        if n == 4:
            m = rowi & 3
            return jnp.where(m == 0, pltpu.roll(b, c - 1, 0),
                             jnp.where(m == 1, b,
                                       jnp.where(m == 2, pltpu.roll(b, 1, 0),
                                                 pltpu.roll(b, 2, 0))))
        pieces = [jnp.broadcast_to(b[base + hb - 1:base + hb, :], (n, LANE))
                  for base in range(0, c, n)]
        return pieces[0] if len(pieces) == 1 else jnp.concatenate(pieces, axis=0)

    def head_chunk(rows, hh):
        cs = slice(hh * LANE, (hh + 1) * LANE)
        lb = lb_ref[:, cs]
        f = lb + (1.0 - lb) * _sigmoid(f_ref[rows, cs])
        kk = 1.0 - f
        qr = q_ref[rows, cs]
        q = qr * _sigmoid(qr)
        iv = i_ref[rows, cs].astype(BF16)
        b = _row_cumsum(jnp.log(f) * LOG2E)
        b_last = b[c - 1:c, :]
        st = st_scr[hh]
        o = dotf((q * jnp.exp2(b)).astype(BF16), st.astype(BF16), NT)
        a = jnp.where(eye, dotf(q.astype(BF16), kk.astype(BF16), NT), 0.0)
        for hb, mask in zip(levels, masks):
            upper = (rowi & hb) != 0
            d = b - level_ref(b, hb)
            x = (jnp.where(upper, q, kk) * jnp.exp2(jnp.where(upper, d, -d))).astype(BF16)
            a = a + jnp.where(mask, dotf(x, x, NT), 0.0)
        o = o + jnp.dot(a.astype(BF16), iv, preferred_element_type=F32)
        kb = kk * jnp.exp2(b_last - b)
        st_scr[hh] = st * jnp.exp2(b_last) + dotf(iv, kb.astype(BF16), TN)
        o = o * lax.rsqrt(jnp.mean(o * o, axis=-1, keepdims=True) + EPS) * ng_ref[:, cs]
        gr = g_ref[rows, cs]
        o_ref[rows, cs] = (o * (gr * _sigmoid(gr))).astype(o_ref.dtype)

    for ci in range(cpb):
        for hh in range(hpb):
            head_chunk(slice(ci * c, (ci + 1) * c), hh)

    @pl.when(j == pl.num_programs(2) - 1)
    def _():
        for hh in range(hpb):
            sout_ref[0, hh] = st_scr[hh].T


def _hgrn(p, row0, nstream, t, heads, lb, norm_g, s0, prev):
    m = p.shape[0]
    c = min(t, HGRN_CHUNK)
    rows = _tile(t, HGRN_ROWS_PER_STEP, c)
    cpb = rows // c
    nj = t // rows
    rb0 = row0 // rows
    hpb = _tile(heads, HGRN_HEADS_PER_STEP, 1)
    ng = heads // hpb
    w = hpb * LANE
    col = lambda part: (lambda b, hg, j: (rb0 + b * nj + j, part * ng + hg))
    vec = pl.BlockSpec((1, w), lambda b, hg, j: (0, hg))
    st_spec = pl.BlockSpec((1, hpb, LANE, LANE), lambda b, hg, j: (b, hg, 0, 0))
    n_prev = 0 if prev is None else 1
    o, s_out = pl.pallas_call(
        _without_refs(functools.partial(_hgrn_kernel, c=c, cpb=cpb, hpb=hpb), 7, n_prev),
        grid=(nstream, ng, nj),
        in_specs=[pl.BlockSpec((rows, w), col(0)), pl.BlockSpec((rows, w), col(1)),
                  pl.BlockSpec((rows, w), col(2)), pl.BlockSpec((rows, w), col(3)),
                  vec, vec, st_spec] + [ANY_SPEC] * n_prev,
        out_specs=[pl.BlockSpec((rows, w), lambda b, hg, j: (rb0 + b * nj + j, hg)), st_spec],
        out_shape=[jax.ShapeDtypeStruct((m, heads * LANE), BF16),
                   jax.ShapeDtypeStruct((nstream, heads, LANE, LANE), F32)],
        scratch_shapes=[pltpu.VMEM((hpb, LANE, LANE), F32)],
        input_output_aliases={7: 0} if prev is not None else {},
        compiler_params=_params("parallel", "parallel", "arbitrary"),
        name="hgrn2",
    )(p, p, p, p, lb.reshape(1, -1), norm_g.reshape(1, -1), s0, *(() if prev is None else (prev,)))
    return o, s_out


def _gmlp_kernel(u_ref, v_ref, ng_ref, w_ref, bst_ref, o_ref, *vout_ref, groups):
    v = v_ref[...]
    vn = v * lax.rsqrt(jnp.mean(v * v, axis=-1, keepdims=True) + EPS) * ng_ref[...]
    if vout_ref:
        vout_ref[0][...] = vn
    c = v.shape[0]
    tril = lax.broadcasted_iota(jnp.int32, (c, c), 0) >= lax.broadcasted_iota(jnp.int32, (c, c), 1)
    for g in range(groups):
        cs = slice(g * LANE, (g + 1) * LANE)
        w = jnp.where(tril, w_ref[g], 0.0).astype(BF16)
        s = jnp.dot(w, vn[:, cs].astype(BF16), preferred_element_type=F32) + bst_ref[:, g:g + 1]
        o_ref[:, cs] = (u_ref[:, cs] * s).astype(o_ref.dtype)


def _gmlp(p, row0, nrows, t, col0, d_b, norm_g, w_s, b_s, prev):
    m = p.shape[0]
    groups = w_s.shape[0]
    c = min(t, GMLP_CHUNK)
    w = w_s[:, :c, :c]
    bst = jnp.transpose(b_s[:, :c])
    rb0 = row0 // c
    cb = col0 // d_b
    emit_v = prev is not None
    out_shape = [jax.ShapeDtypeStruct((m, d_b), BF16)]
    out_specs = [pl.BlockSpec((c, d_b), lambda i: (rb0 + i, 0))]
    if emit_v:
        out_shape.append(jax.ShapeDtypeStruct((nrows, d_b), F32))
        out_specs.append(pl.BlockSpec((c, d_b), lambda i: (i, 0)))
    n_prev = 0 if prev is None else 1
    return pl.pallas_call(
        _without_refs(functools.partial(_gmlp_kernel, groups=groups), 5, n_prev),
        grid=(nrows // c,),
        in_specs=[pl.BlockSpec((c, d_b), lambda i: (rb0 + i, cb)),
                  pl.BlockSpec((c, d_b), lambda i: (rb0 + i, cb + 1)),
                  pl.BlockSpec((1, d_b), lambda i: (0, 0)),
                  pl.BlockSpec((groups, c, c), lambda i: (0, 0, 0)),
                  pl.BlockSpec((c, groups), lambda i: (0, 0))] + [ANY_SPEC] * n_prev,
        out_specs=out_specs,
        out_shape=out_shape,
        input_output_aliases={5: 0} if prev is not None else {},
        compiler_params=_params("parallel"),
        name="gmlp",
    )(p, p, norm_g.reshape(1, d_b), w, bst, *(() if prev is None else (prev,)))


def _log_sigmoid(z):
    return jnp.minimum(z, 0.0) - jnp.log1p(jnp.exp(-jnp.abs(z)))


def _fox_prep_kernel(cf_ref, bias_ref, logf_ref, f_ref, ft_ref, carry, *, heads):
    @pl.when(pl.program_id(1) == 0)
    def _():
        carry[...] = jnp.zeros_like(carry)

    logf = _log_sigmoid(cf_ref[...] + bias_ref[...])
    logf_ref[...] = logf[:, :heads]
    fc = carry[...] + _row_cumsum(logf)
    carry[...] = fc[fc.shape[0] - 1:, :]
    fc = fc * LOG2E
    f_ref[...] = fc
    ft_ref[0] = fc.T[:heads, :]


def _fox_prep(cf, bias, nstream, t, heads):
    tb = _tile(t, 256, LANE)
    nj = t // tb
    return pl.pallas_call(
        functools.partial(_fox_prep_kernel, heads=heads),
        grid=(nstream, nj),
        in_specs=[pl.BlockSpec((tb, LANE), lambda b, j: (b * nj + j, 0)),
                  pl.BlockSpec((1, LANE), lambda b, j: (0, 0))],
        out_specs=[pl.BlockSpec((tb, heads), lambda b, j: (b * nj + j, 0)),
                   pl.BlockSpec((tb, LANE), lambda b, j: (b * nj + j, 0)),
                   pl.BlockSpec((1, heads, tb), lambda b, j: (b, 0, j))],
        out_shape=[jax.ShapeDtypeStruct((nstream * t, heads), F32),
                   jax.ShapeDtypeStruct((nstream * t, LANE), F32),
                   jax.ShapeDtypeStruct((nstream, heads, t), F32)],
        scratch_shapes=[pltpu.VMEM((1, LANE), F32)],
        compiler_params=_params("parallel", "arbitrary"),
        name="fox_prep",
    )(cf, bias)


def _fox_attn_kernel(qi_ref, kj_ref, q_ref, k_ref, vt_ref, fq_ref, fk_ref, o_ref,
                     m_scr, l_scr, acc_scr, *, heads, blk, ks):
    t = pl.program_id(1)
    i = qi_ref[t]
    j = kj_ref[t]
    nsub = blk // ks

    @pl.when(j == 0)
    def _():
        m_scr[...] = jnp.full_like(m_scr, -jnp.inf)
        l_scr[...] = jnp.zeros_like(l_scr)
        acc_scr[...] = jnp.zeros_like(acc_scr)

    def step(diagonal):
        if diagonal:
            key = lax.broadcasted_iota(jnp.int32, (ks, blk), 0)
            qry = lax.broadcasted_iota(jnp.int32, (ks, blk), 1)
            hide = [jnp.where(qry >= key + u * ks, 0.0, -jnp.inf) for u in range(nsub)]
        for h in range(heads):
            cs = slice(h * LANE, (h + 1) * LANE)
            fq = fq_ref[0, h:h + 1, :]
            q = q_ref[:, cs]
            for u in range(nsub):
                rs = slice(u * ks, (u + 1) * ks)
                z = lax.dot_general(k_ref[rs, cs], q, NT,
                                    preferred_element_type=F32) - fk_ref[rs, h:h + 1]
                if diagonal:
                    z = z + hide[u]
                m_prev = m_scr[h]
                m_new = jnp.maximum(m_prev, jnp.max(z, axis=0, keepdims=True) + fq)
                alpha = jnp.exp2(m_prev - m_new)
                p = jnp.exp2(z + (fq - m_new))
                l_scr[h] = alpha * l_scr[h] + jnp.sum(p, axis=0, keepdims=True)
                acc_scr[h] = alpha * acc_scr[h] + jnp.dot(
                    vt_ref[cs, rs], p.astype(BF16), preferred_element_type=F32)
                m_scr[h] = m_new

    @pl.when(j < i)
    def _():
        step(False)

    @pl.when(j == i)
    def _():
        step(True)
        for h in range(heads):
            cs = slice(h * LANE, (h + 1) * LANE)
            o_ref[:, cs] = (acc_scr[h] / l_scr[h]).T.astype(o_ref.dtype)


def _fox_attn_prompt(q, k, vt, m_total, nstream, t, heads, f_col, f_row):
    d_c = heads * LANE
    blk = _tile(t, ATTN_BLOCK, LANE)
    ks = _tile(blk, ATTN_KEY_SUB, LANE)
    nb = t // blk
    pairs = [(i, j) for i in range(nb) for j in range(i + 1)]
    qi = jnp.array([p[0] for p in pairs], jnp.int32)
    kj = jnp.array([p[1] for p in pairs], jnp.int32)
    q_row = lambda b, s, qi, kj: b * nb + qi[s]
    k_row = lambda b, s, qi, kj: b * nb + kj[s]
    stat = pltpu.VMEM((heads, 1, blk), F32)
    return pl.pallas_call(
        functools.partial(_fox_attn_kernel, heads=heads, blk=blk, ks=ks),
        grid_spec=pltpu.PrefetchScalarGridSpec(
            num_scalar_prefetch=2,
            grid=(nstream, len(pairs)),
            in_specs=[pl.BlockSpec((blk, d_c), lambda b, s, qi, kj: (q_row(b, s, qi, kj), 0)),
                      pl.BlockSpec((blk, d_c), lambda b, s, qi, kj: (k_row(b, s, qi, kj), 0)),
                      pl.BlockSpec((d_c, blk), lambda b, s, qi, kj: (0, k_row(b, s, qi, kj))),
                      pl.BlockSpec((1, heads, blk), lambda b, s, qi, kj: (b, 0, qi[s])),
                      pl.BlockSpec((blk, LANE), lambda b, s, qi, kj: (k_row(b, s, qi, kj), 0))],
            out_specs=pl.BlockSpec((blk, d_c), lambda b, s, qi, kj: (q_row(b, s, qi, kj), 0)),
            scratch_shapes=[stat, stat, pltpu.VMEM((heads, LANE, blk), F32)]),
        out_shape=jax.ShapeDtypeStruct((m_total, d_c), BF16),
        compiler_params=_params("parallel", "arbitrary"),
        name="fox_attn_prompt",
    )(qi, kj, q, k, vt, f_row, f_col)


def _fox_sample_kernel(q_ref, k_ref, v_ref, cf_ref, bias_ref, pk_ref, pv_ref, plf_ref,
                       o_ref, lf_ref, *, heads):
    ts = q_ref.shape[0]
    past = pk_ref.shape[1]
    f_past = _row_cumsum(plf_ref[0])
    logf = _log_sigmoid(cf_ref[...] + bias_ref[...])
    lf_ref[0] = logf[:, :heads]
    f_new = f_past[past - 1:past, :] + _row_cumsum(logf)
    f_past = f_past * LOG2E
    f_new = f_new * LOG2E
    f_past_t = f_past.T
    f_new_t = jnp.concatenate([f_new, jnp.zeros((LANE - ts, LANE), F32)], axis=0).T
    tril = lax.broadcasted_iota(jnp.int32, (ts, ts), 0) >= lax.broadcasted_iota(jnp.int32, (ts, ts), 1)
    dotf = functools.partial(lax.dot_general, preferred_element_type=F32)
    for h in range(heads):
        cs = slice(h * LANE, (h + 1) * LANE)
        qh = q_ref[:, cs]
        fq = f_new[:, h:h + 1]
        s_p = dotf(qh, pk_ref[0, :, cs].astype(BF16), NT) + fq - f_past_t[h:h + 1, :]
        s_n = dotf(qh, k_ref[:, cs], NT) + fq - f_new_t[h:h + 1, :ts]
        s_n = jnp.where(tril, s_n, -jnp.inf)
        m = jnp.maximum(jnp.max(s_p, axis=-1, keepdims=True), jnp.max(s_n, axis=-1, keepdims=True))
        e_p = jnp.exp2(s_p - m)
        e_n = jnp.exp2(s_n - m)
        l = jnp.sum(e_p, axis=-1, keepdims=True) + jnp.sum(e_n, axis=-1, keepdims=True)
        acc = (jnp.dot(e_p.astype(BF16), pv_ref[0, :, cs].astype(BF16), preferred_element_type=F32)
               + jnp.dot(e_n.astype(BF16), v_ref[:, cs], preferred_element_type=F32))
        o_ref[:, cs] = (acc / l).astype(o_ref.dtype)


def _fox_attn_sample(q, row0, k_new, v_new, cf, bias, past_k, past_v, past_logf, l,
                     nstream, ts, heads, prev):
    d_c = heads * LANE
    past = past_k.shape[2]
    rb0 = row0 // ts
    new = pl.BlockSpec((ts, d_c), lambda b: (b, 0))
    cache = pl.BlockSpec((None, 1, past, d_c), lambda b: (l, b, 0, 0))
    return pl.pallas_call(
        _without_refs(functools.partial(_fox_sample_kernel, heads=heads), 8, 1),
        grid=(nstream,),
        in_specs=[pl.BlockSpec((ts, d_c), lambda b: (rb0 + b, 0)), new, new,
                  pl.BlockSpec((ts, LANE), lambda b: (rb0 + b, 0)),
                  pl.BlockSpec((1, LANE), lambda b: (0, 0)),
                  cache, cache,
                  pl.BlockSpec((None, 1, past, LANE), lambda b: (l, b, 0, 0)),
                  ANY_SPEC],
        out_specs=[pl.BlockSpec((ts, d_c), lambda b: (rb0 + b, 0)),
                   pl.BlockSpec((1, ts, heads), lambda b: (b, 0, 0))],
        out_shape=[jax.ShapeDtypeStruct(prev.shape, BF16),
                   jax.ShapeDtypeStruct((nstream, ts, heads), F32)],
        input_output_aliases={8: 0},
        compiler_params=_params("parallel"),
        name="fox_attn_sample",
    )(q, k_new, v_new, cf, bias, past_k, past_v, past_logf, prev)


def kernel(x_prompt, x_sample, state_hgrn, cache_k, cache_v, cache_logf, ffn1_norm, ffn1_w_in,
           ffn1_w_out, mix_norm, w_in, w_gate, hgrn_lb_logits, hgrn_norm, gmlp_norm, gmlp_w_s,
           gmlp_b_s, fox_bias, w_branch_a, w_branch_b, w_branch_c, w_out, ffn2_norm, ffn2_w_in,
           ffn2_w_out, final_norm):
    nb, seq, d = x_prompt.shape
    ns, ts, _ = x_sample.shape
    depth = state_hgrn.shape[0]
    a_heads = state_hgrn.shape[2]
    c_heads = cache_k.shape[3]
    past = cache_k.shape[2]
    d_a = a_heads * LANE
    d_b = gmlp_norm.shape[1]
    d_c = c_heads * LANE
    col_b = 4 * d_a
    col_q = col_b + 2 * d_b
    col_k = col_q + d_c
    n_main = col_k + 2 * d_c
    mp = nb * seq
    ms = ns * ts
    assert w_in.shape[2] == n_main + c_heads and c_heads <= LANE
    assert col_b % d_b == 0 and mp % ts == 0

    bf = lambda w: w.astype(BF16)
    w1_in, w2_in = ffn1_w_in, ffn2_w_in
    w1_out, w2_out = bf(ffn1_w_out), bf(ffn2_w_out)
    w_in_b, w_gate_b, w_out_b = bf(w_in), bf(w_gate), bf(w_out)
    wa, wb, wc = bf(w_branch_a), bf(w_branch_b), bf(w_branch_c)
    w_cf = bf(jnp.pad(w_in[:, :, n_main:], ((0, 0), (0, 0), (0, LANE - c_heads))))
    bias_all = jnp.pad(fox_bias.astype(F32), ((0, 0), (0, LANE - c_heads)))
    past_k = cache_k.reshape(depth, ns, past, d_c)
    past_v = cache_v.reshape(depth, ns, past, d_c)
    past_lf = jnp.pad(cache_logf.astype(F32), ((0, 0), (0, 0), (0, 0), (0, LANE - c_heads)))

    lower_bounds = _lower_bounds(hgrn_lb_logits)
    x = jnp.concatenate([x_prompt.reshape(mp, d), x_sample.reshape(ms, d)], axis=0)
    s0_prompt = jnp.zeros((nb, a_heads, LANE, LANE), F32)

    kv_p = kv_s = None
    outs = {name: [] for name in ("sa_p", "sa_s", "vb_s", "lf_p", "lf_s")}
    for l in range(depth):
        bias = bias_all[l:l + 1]

        hid = _ffn_up(_rmsnorm(x, ffn1_norm[l], BF16), w1_in, l)
        h = _mm_res(hid, w1_out, l, x, 0.5, 256)

        n = _rmsnorm(h, mix_norm[l], BF16)
        p_ab = _mm(n, w_in_b, l, 0, col_q, F32)
        q_c = _mm(n, w_in_b, l, col_q, d_c, BF16, out_scale=LANE ** -0.5 * LOG2E)
        g = _mm(n, w_gate_b, l, 0, 3 * d, F32, gate=True)
        cf = _mm(n, w_cf, l, 0, LANE, F32)
        kp_all, vp_all, kp, vpt = _kv_proj(n, w_in_b, l, col_k, d_c, 0, mp, depth, kv_p, True)
        ks_all, vs_all, ks, vs = _kv_proj(n, w_in_b, l, col_k, d_c, mp, ms, depth, kv_s, False)
        kv_p, kv_s = (kp_all, vp_all), (ks_all, vs_all)

        oa, sa_p = _hgrn(p_ab, 0, nb, seq, a_heads, lower_bounds[l], hgrn_norm[l], s0_prompt, None)
        oa, sa_s = _hgrn(p_ab, mp, ns, ts, a_heads, lower_bounds[l], hgrn_norm[l], state_hgrn[l], oa)
        (ob,) = _gmlp(p_ab, 0, mp, seq, col_b, d_b, gmlp_norm[l], gmlp_w_s[l], gmlp_b_s[l], None)
        ob, vb_s = _gmlp(p_ab, mp, ms, ts, col_b, d_b, gmlp_norm[l], gmlp_w_s[l], gmlp_b_s[l], ob)
        lf_p, f_col, f_row = _fox_prep(cf, bias, nb, seq, c_heads)
        oc = _fox_attn_prompt(q_c, kp, vpt, mp + ms, nb, seq, c_heads, f_col, f_row)
        oc, lf_s = _fox_attn_sample(q_c, mp, ks, vs, cf, bias, past_k, past_v, past_lf, l,
                                    ns, ts, c_heads, oc)

        merged = _branch_merge(oa, ob, oc, wa, wb, wc, l, g)
        h = _mm_res(merged, w_out_b, l, h, 1.0, 1024)

        hid = _ffn_up(_rmsnorm(h, ffn2_norm[l], BF16), w2_in, l)
        x = _mm_res(hid, w2_out, l, h, 0.5, 256)

        outs["sa_p"].append(sa_p)
        outs["sa_s"].append(sa_s)
        outs["vb_s"].append(vb_s.reshape(ns, ts, d_b))
        outs["lf_p"].append(lf_p.reshape(nb, seq, c_heads))
        outs["lf_s"].append(lf_s)

    y_p = _rmsnorm(x, final_norm, F32, 0, mp)
    y_s = _rmsnorm(x, final_norm, F32, mp, ms)
    stack = lambda name: jnp.stack(outs[name])
    return (y_p.reshape(nb, seq, d), y_s.reshape(ns, ts, d),
            stack("sa_p"), stack("sa_s"), stack("vb_s"),
            kv_p[0].reshape(depth, nb, seq, c_heads, LANE), kv_s[0].reshape(depth, ns, ts, c_heads, LANE),
            kv_p[1].reshape(depth, nb, seq, c_heads, LANE), kv_s[1].reshape(depth, ns, ts, c_heads, LANE),
            stack("lf_p"), stack("lf_s"))
```
